```python
import math
import jax, jax.numpy as jnp
from jax import lax
import numpy as np

D_MODEL = 1024
BATCH = 4
SEQ = 8192
DEPTH = 2

N_MEM = 256
D_HYENA = 256
RW_HEADS = 6
RW_HEAD_DIM = 64
D_RWKV = RW_HEADS * RW_HEAD_DIM
GDN_HEADS = 6
GDN_HEAD_DIM = 64
D_GDN = GDN_HEADS * GDN_HEAD_DIM
D_MIX = D_HYENA + D_RWKV + D_GDN
HY_SHORT = 3
HY_BANDS = 16
HY_EMB = 1 + 2 * HY_BANDS
HY_FFN = 64
HY_FAST_DECAY = 0.3
HY_SLOW_DECAY = 1.5
HY_TARGET = 1e-2
RW_LORA_W = 64
RW_LORA_A = 64
RW_LORA_G = 128
RW_DECAY_SCALE = 0.606531
RW_GN_EPS = 64e-5
GDN_SHORT = 3
GDN_CHUNK = 64
XA_HEADS = 4
XA_HEAD_DIM = D_MODEL // XA_HEADS
D_FF = 2816
NORM_EPS = 1e-6
HY_COLS = 3 * D_HYENA
RW_COLS = 3 * D_RWKV + 2 * RW_LORA_W + 2 * RW_LORA_A + RW_LORA_G
GDN_COLS = 4 * D_GDN + 4 * GDN_HEADS
IN_COLS = HY_COLS + RW_COLS + GDN_COLS

kernel_name = "hybrid_hyena_rwkv7_gdn_encoder"


def _split(t, sizes):
    return jnp.split(t, np.cumsum(sizes)[:-1].tolist(), axis=-1)


def rmsnorm(x, g, eps=NORM_EPS):
    xf = x.astype(jnp.float32)
    y = xf * lax.rsqrt(jnp.mean(xf * xf, axis=-1, keepdims=True) + eps)
    return (y * g.astype(jnp.float32)).astype(x.dtype)


def l2norm_heads(t, heads, eps=1e-6):
    B_, L, D = t.shape
    th = t.reshape(B_, L, heads, D // heads)
    th = th * lax.rsqrt(jnp.sum(th * th, axis=-1, keepdims=True) + eps)
    return th.reshape(B_, L, D)


def swiglu(x, w1, w3, w2):
    return (jax.nn.silu(x @ w1) * (x @ w3)) @ w2


def depthwise_conv(u, w):
    K = w.shape[0]
    return lax.conv_general_dilated(u, w[:, None, :].astype(u.dtype), window_strides=(1,),
                                    padding=[(K // 2, K // 2)],
                                    dimension_numbers=("NWC", "WIO", "NWC"),
                                    feature_group_count=u.shape[-1])


def token_shift(u, mu_prev, mu_next):
    prev = jnp.pad(u, ((0, 0), (1, 0), (0, 0)))[:, :-1]
    nxt = jnp.pad(u, ((0, 0), (0, 1), (0, 0)))[:, 1:]
    return u + mu_prev * (prev - u) + mu_next * (nxt - u)


def hyena_positional_features(L):
    t = jnp.linspace(0.0, 1.0, L, dtype=jnp.float32)[:, None]
    ang = 2.0 * math.pi * jnp.arange(L, dtype=jnp.float32)[:, None] / L
    bands = jnp.linspace(1e-4, HY_BANDS - 1, HY_BANDS, dtype=jnp.float32)[None, :]
    z = jnp.concatenate([t, jnp.cos(bands * ang), -jnp.sin(bands * ang)], axis=-1)
    return z, t


def hyena_filter(z, t, freq, w1, b1, w2, b2, w3, decay):
    f32 = jnp.float32
    freq = freq.astype(f32)
    h = jnp.sin(freq * (z @ w1.astype(f32) + b1.astype(f32)))
    h = jnp.sin(freq * (h @ w2.astype(f32) + b2.astype(f32)))
    h = (h @ w3.astype(f32)) * jnp.exp(-t * decay.astype(f32))
    h_fwd, h_bwd = h[:, :D_HYENA], h[:, D_HYENA:]
    kern = jnp.concatenate([h_fwd, jnp.zeros_like(h_fwd[:1]), h_bwd[:0:-1]], axis=0)
    return kern / jnp.sum(jnp.abs(kern), axis=0, keepdims=True)


def two_sided_long_conv(u, kern):
    L = u.shape[1]
    uf = jnp.fft.rfft(u.astype(jnp.float32), n=2 * L, axis=1)
    kf = jnp.fft.rfft(kern, n=2 * L, axis=0)
    return jnp.fft.irfft(uf * kf[None], n=2 * L, axis=1)[:, :L]


def hyena_mixer(p, kern, conv_w, conv_b, bias):
    p = depthwise_conv(p, conv_w) + conv_b
    x0, x1, v = _split(p, (D_HYENA, D_HYENA, D_HYENA))
    u = x1 * v
    y = two_sided_long_conv(u, kern).astype(u.dtype) + bias * u
    return x0 * y


def rwkv7_scan(r, w, k, v, a, b, reverse):
    B_, L, D = r.shape

    def to_heads(t):
        return t.reshape(B_, L, RW_HEADS, RW_HEAD_DIM).transpose(1, 0, 2, 3)

    def step(S, inp):
        r_t, w_t, k_t, v_t, a_t, b_t = inp
        sa = jnp.einsum("bhij,bhj->bhi", S, a_t)
        S = S * w_t[:, :, None, :] + sa[..., None] * b_t[:, :, None, :] + v_t[..., None] * k_t[:, :, None, :]
        return S, jnp.einsum("bhij,bhj->bhi", S, r_t)

    S0 = jnp.zeros((B_, RW_HEADS, RW_HEAD_DIM, RW_HEAD_DIM), jnp.float32)
    _, ys = lax.scan(step, S0, tuple(to_heads(t) for t in (r, w, k, v, a, b)), reverse=reverse)
    return ys.transpose(1, 0, 2, 3).reshape(B_, L, D)


def head_groupnorm(y, w, b, heads, eps):
    B_, L, D = y.shape
    yh = y.reshape(B_, L, heads, D // heads)
    mu = jnp.mean(yh, axis=-1, keepdims=True)
    var = jnp.mean(jnp.square(yh - mu), axis=-1, keepdims=True)
    return ((yh - mu) * lax.rsqrt(var + eps)).reshape(B_, L, D) * w + b


def rwkv7_mixer(p, mu_prev, mu_next, w_lora, w0, a_lora, a0, g_lora, k_k, k_a, r_k, gn_w, gn_b):
    f32 = jnp.float32
    p = token_shift(p, mu_prev, mu_next).astype(f32)
    r, k, v, lw_f, lw_b, la_f, la_b, lg = _split(
        p, (D_RWKV, D_RWKV, D_RWKV, RW_LORA_W, RW_LORA_W, RW_LORA_A, RW_LORA_A, RW_LORA_G))
    gate = jax.nn.sigmoid(lg) @ g_lora.astype(f32)
    kk = l2norm_heads(k * k_k, RW_HEADS)
    y = 0.0
    for d, (lw, la, rev) in enumerate(((lw_f, la_f, False), (lw_b, la_b, True))):
        log_w = -RW_DECAY_SCALE * jax.nn.sigmoid(w0[d] + jnp.tanh(lw) @ w_lora[d].astype(f32))
        a = jax.nn.sigmoid(a0[d] + la @ a_lora[d].astype(f32))
        k_d = k * (1.0 + (a - 1.0) * k_a)
        y = y + rwkv7_scan(r, jnp.exp(log_w), k_d, v, -kk, kk * a, rev)
    y = head_groupnorm(y, gn_w, gn_b, RW_HEADS, RW_GN_EPS)
    B_, L, _ = r.shape
    bonus = jnp.sum((r * k * r_k).reshape(B_, L, RW_HEADS, RW_HEAD_DIM), axis=-1, keepdims=True)
    y = y + (bonus * v.reshape(B_, L, RW_HEADS, RW_HEAD_DIM)).reshape(B_, L, D_RWKV)
    return y * gate


def chunk_gated_delta(q, k, v, g, beta):
    B_, L, H, N = q.shape
    C = GDN_CHUNK
    n = L // C

    def chunks(t):
        return t.reshape(B_, n, C, H, -1).transpose(0, 3, 1, 2, 4)

    def chunks_s(t):
        return t.reshape(B_, n, C, H).transpose(0, 3, 1, 2)

    qc, kc, vc = chunks(q), chunks(k), chunks(v)
    gc = jnp.cumsum(chunks_s(g), axis=-1)
    bc = chunks_s(beta)
    idx = jnp.arange(C)
    causal = idx[:, None] >= idx[None, :]
    decay = jnp.exp(jnp.where(causal, gc[..., :, None] - gc[..., None, :], -jnp.inf))
    kb = kc * bc[..., None]
    lower = jnp.einsum("bhncd,bhnsd->bhncs", kb, kc) * decay
    rhs = jnp.concatenate([vc * bc[..., None], kb * jnp.exp(gc)[..., None]], axis=-1)
    sol = lax.linalg.triangular_solve(lower, rhs, left_side=True, lower=True, unit_diagonal=True)
    u_c, w_c = sol[..., :N], sol[..., N:]
    attn = jnp.einsum("bhncd,bhnsd->bhncs", qc, kc) * decay
    q_dec = qc * jnp.exp(gc)[..., None]
    g_last = gc[..., -1]
    k_end = kc * jnp.exp(g_last[..., None] - gc)[..., None]

    def step(S, inp):
        u_i, w_i, attn_i, q_i, k_i, gl = inp
        v_new = u_i - jnp.einsum("bhcd,bhde->bhce", w_i, S)
        o_i = jnp.einsum("bhcd,bhde->bhce", q_i, S) + jnp.einsum("bhcs,bhse->bhce", attn_i, v_new)
        S = S * jnp.exp(gl)[..., None, None] + jnp.einsum("bhcd,bhce->bhde", k_i, v_new)
        return S, o_i

    xs = tuple(jnp.moveaxis(t, 2, 0) for t in (u_c, w_c, attn, q_dec, k_end, g_last))
    S0 = jnp.zeros((B_, H, N, N), jnp.float32)
    _, o = lax.scan(step, S0, xs)
    return o.transpose(1, 0, 3, 2, 4).reshape(B_, L, H, N)


def gdn_mixer(p, conv_w, a_log, dt_bias, norm_w):
    f32 = jnp.float32
    qkv, zg, a_f, a_b, b_f, b_b = _split(
        p, (3 * D_GDN, D_GDN, GDN_HEADS, GDN_HEADS, GDN_HEADS, GDN_HEADS))
    qkv = jax.nn.silu(depthwise_conv(qkv, conv_w)).astype(f32)
    q, k, v = _split(qkv, (D_GDN, D_GDN, D_GDN))
    B_, L, _ = q.shape
    q = l2norm_heads(q, GDN_HEADS) * (GDN_HEAD_DIM ** -0.5)
    k = l2norm_heads(k, GDN_HEADS)
    hd = lambda t: t.reshape(B_, L, GDN_HEADS, GDN_HEAD_DIM)
    q, k, v = hd(q), hd(k), hd(v)
    o = 0.0
    for d, (ag, bg, rev) in enumerate(((a_f, b_f, False), (a_b, b_b, True))):
        g = -jnp.exp(a_log[d].astype(f32)) * jax.nn.softplus(ag.astype(f32) + dt_bias[d])
        beta = jax.nn.sigmoid(bg.astype(f32))
        if rev:
            o_d = chunk_gated_delta(q[:, ::-1], k[:, ::-1], v[:, ::-1], g[:, ::-1], beta[:, ::-1])[:, ::-1]
        else:
            o_d = chunk_gated_delta(q, k, v, g, beta)
        o = o + o_d
    o = o * lax.rsqrt(jnp.mean(o * o, axis=-1, keepdims=True) + NORM_EPS) * norm_w
    o = o * jax.nn.silu(hd(zg.astype(f32)))
    return o.reshape(B_, L, D_GDN)


def memory_cross_attention(h, mem_n, wq, wk, wv, wo):
    B_, L, _ = h.shape
    M = mem_n.shape[1]
    q = (h @ wq).reshape(B_, L, XA_HEADS, XA_HEAD_DIM)
    k = (mem_n @ wk).reshape(B_, M, XA_HEADS, XA_HEAD_DIM)
    v = (mem_n @ wv).reshape(B_, M, XA_HEADS, XA_HEAD_DIM)
    s = jnp.einsum("blhd,bmhd->bhlm", q, k).astype(jnp.float32) * (XA_HEAD_DIM ** -0.5)
    pr = jax.nn.softmax(s, axis=-1).astype(v.dtype)
    o = jnp.einsum("bhlm,bmhd->blhd", pr, v).reshape(B_, L, D_MODEL)
    return o @ wo


def setup_inputs(seed: int = 0) -> dict:
    key = jax.random.key(seed)
    ks = iter(jax.random.split(key, 96))
    f32 = jnp.float32

    def nrm(shape, scale):
        return scale * jax.random.normal(next(ks), shape, f32)

    def gain(shape):
        return 1.0 + 0.05 * jax.random.normal(next(ks), shape, f32)

    def unif(shape, lo, hi):
        return jax.random.uniform(next(ks), shape, f32, minval=lo, maxval=hi)

    hy_rates = jnp.tile(jnp.linspace(-math.log(HY_TARGET) / HY_SLOW_DECAY,
                                     -math.log(HY_TARGET) / HY_FAST_DECAY, D_HYENA, dtype=f32), 2)
    dt = jnp.exp(unif((DEPTH, 2, GDN_HEADS), math.log(1e-3), math.log(1e-1)))
    return {
        "x": nrm((BATCH, SEQ, D_MODEL), 1.0),
        "mem": nrm((BATCH, N_MEM, D_MODEL), 1.0),
        "norm_ffn1": gain((DEPTH, D_MODEL)),
        "ffn1_w1": nrm((DEPTH, D_MODEL, D_FF), D_MODEL ** -0.5),
        "ffn1_w3": nrm((DEPTH, D_MODEL, D_FF), D_MODEL ** -0.5),
        "ffn1_w2": nrm((DEPTH, D_FF, D_MODEL), D_FF ** -0.5),
        "norm_mix": gain((DEPTH, D_MODEL)),
        "w_in": nrm((DEPTH, D_MODEL, IN_COLS), D_MODEL ** -0.5),
        "w_out": nrm((DEPTH, D_MIX, D_MODEL), D_MIX ** -0.5),
        "hy_conv_w": nrm((DEPTH, HY_SHORT, HY_COLS), HY_SHORT ** -0.5),
        "hy_conv_b": nrm((DEPTH, HY_COLS), 0.02),
        "hy_freq": gain((DEPTH, HY_FFN)),
        "hy_w1": nrm((DEPTH, HY_EMB, HY_FFN), HY_EMB ** -0.5),
        "hy_b1": nrm((DEPTH, HY_FFN), 0.1),
        "hy_w2": nrm((DEPTH, HY_FFN, HY_FFN), HY_FFN ** -0.5),
        "hy_b2": nrm((DEPTH, HY_FFN), 0.1),
        "hy_w3": nrm((DEPTH, HY_FFN, 2 * D_HYENA), HY_FFN ** -0.5),
        "hy_decay": hy_rates * gain((DEPTH, 2 * D_HYENA)),
        "hy_bias": nrm((DEPTH, D_HYENA), 1.0),
        "rw_mu_prev": unif((DEPTH, RW_COLS), 0.0, 0.5),
        "rw_mu_next": unif((DEPTH, RW_COLS), 0.0, 0.5),
        "rw_w_lora": nrm((DEPTH, 2, RW_LORA_W, D_RWKV), 0.5 * RW_LORA_W ** -0.5),
        "rw_w0": jnp.linspace(-5.0, -0.5, D_RWKV, dtype=f32) + nrm((DEPTH, 2, D_RWKV), 0.1),
        "rw_a_lora": nrm((DEPTH, 2, RW_LORA_A, D_RWKV), 0.5 * RW_LORA_A ** -0.5),
        "rw_a0": nrm((DEPTH, 2, D_RWKV), 0.1),
        "rw_g_lora": nrm((DEPTH, RW_LORA_G, D_RWKV), RW_LORA_G ** -0.5),
        "rw_k_k": 0.85 + nrm((DEPTH, D_RWKV), 0.05),
        "rw_k_a": gain((DEPTH, D_RWKV)),
        "rw_r_k": nrm((DEPTH, D_RWKV), 0.1),
        "rw_gn_w": gain((DEPTH, D_RWKV)),
        "rw_gn_b": nrm((DEPTH, D_RWKV), 0.02),
        "gdn_conv_w": nrm((DEPTH, GDN_SHORT, 3 * D_GDN), GDN_SHORT ** -0.5),
        "gdn_a_log": jnp.log(unif((DEPTH, 2, GDN_HEADS), 1.0, 16.0)),
        "gdn_dt_bias": dt + jnp.log(-jnp.expm1(-dt)),
        "gdn_norm_w": gain((DEPTH, GDN_HEAD_DIM)),
        "norm_xattn": gain((DEPTH, D_MODEL)),
        "xa_wq": nrm((DEPTH, D_MODEL, D_MODEL), D_MODEL ** -0.5),
        "xa_wk": nrm((DEPTH, D_MODEL, D_MODEL), D_MODEL ** -0.5),
        "xa_wv": nrm((DEPTH, D_MODEL, D_MODEL), D_MODEL ** -0.5),
        "xa_wo": nrm((DEPTH, D_MODEL, D_MODEL), D_MODEL ** -0.5),
        "mem_norm": gain((D_MODEL,)),
        "norm_ffn2": gain((DEPTH, D_MODEL)),
        "ffn2_w1": nrm((DEPTH, D_MODEL, D_FF), D_MODEL ** -0.5),
        "ffn2_w3": nrm((DEPTH, D_MODEL, D_FF), D_MODEL ** -0.5),
        "ffn2_w2": nrm((DEPTH, D_FF, D_MODEL), D_FF ** -0.5),
        "norm_final": gain((D_MODEL,)),
    }


def reference(x, mem, norm_ffn1, ffn1_w1, ffn1_w3, ffn1_w2, norm_mix, w_in, w_out,
              hy_conv_w, hy_conv_b, hy_freq, hy_w1, hy_b1, hy_w2, hy_b2, hy_w3, hy_decay, hy_bias,
              rw_mu_prev, rw_mu_next, rw_w_lora, rw_w0, rw_a_lora, rw_a0, rw_g_lora,
              rw_k_k, rw_k_a, rw_r_k, rw_gn_w, rw_gn_b,
              gdn_conv_w, gdn_a_log, gdn_dt_bias, gdn_norm_w,
              norm_xattn, xa_wq, xa_wk, xa_wv, xa_wo, mem_norm,
              norm_ffn2, ffn2_w1, ffn2_w3, ffn2_w2, norm_final):
    L = x.shape[1]
    z_pos, t_pos = hyena_positional_features(L)
    mem_n = rmsnorm(mem, mem_norm)
    for l in range(DEPTH):
        x = x + 0.5 * swiglu(rmsnorm(x, norm_ffn1[l]), ffn1_w1[l], ffn1_w3[l], ffn1_w2[l])
        h = rmsnorm(x, norm_mix[l])
        p_hy, p_rw, p_gdn = _split(h @ w_in[l], (HY_COLS, RW_COLS, GDN_COLS))
        kern = hyena_filter(z_pos, t_pos, hy_freq[l], hy_w1[l], hy_b1[l], hy_w2[l], hy_b2[l],
                            hy_w3[l], hy_decay[l])
        y_hy = hyena_mixer(p_hy, kern, hy_conv_w[l], hy_conv_b[l], hy_bias[l])
        y_rw = rwkv7_mixer(p_rw, rw_mu_prev[l], rw_mu_next[l], rw_w_lora[l], rw_w0[l], rw_a_lora[l],
                           rw_a0[l], rw_g_lora[l], rw_k_k[l], rw_k_a[l], rw_r_k[l], rw_gn_w[l], rw_gn_b[l])
        y_gdn = gdn_mixer(p_gdn, gdn_conv_w[l], gdn_a_log[l], gdn_dt_bias[l], gdn_norm_w[l])
        y = jnp.concatenate([y_hy.astype(x.dtype), y_rw.astype(x.dtype), y_gdn.astype(x.dtype)], axis=-1)
        x = x + y @ w_out[l]
        x = x + memory_cross_attention(rmsnorm(x, norm_xattn[l]), mem_n, xa_wq[l], xa_wk[l], xa_wv[l], xa_wo[l])
        x = x + 0.5 * swiglu(rmsnorm(x, norm_ffn2[l]), ffn2_w1[l], ffn2_w3[l], ffn2_w2[l])
    return rmsnorm(x, norm_final)
```

```python
import functools
import math

import numpy as np
import jax
import jax.numpy as jnp
from jax import lax
from jax.experimental import pallas as pl
from jax.experimental.pallas import tpu as pltpu

F32 = jnp.float32
BF16 = jnp.bfloat16

D_MODEL = 1024
D_HYENA = 256
RW_HEADS = 6
RW_HEAD_DIM = 64
D_RWKV = RW_HEADS * RW_HEAD_DIM
GDN_HEADS = 6
GDN_HEAD_DIM = 64
D_GDN = GDN_HEADS * GDN_HEAD_DIM
HY_BANDS = 16
HY_EMB = 1 + 2 * HY_BANDS
HY_FFN = 64
RW_LORA_W = 64
RW_LORA_A = 64
RW_LORA_G = 128
RW_DECAY_SCALE = 0.606531
RW_GN_EPS = 64e-5
XA_HEADS = 4
XA_HEAD_DIM = D_MODEL // XA_HEADS
D_FF = 2816
NORM_EPS = 1e-6
L2_EPS = 1e-6
HY_COLS = 3 * D_HYENA
RW_COLS = 3 * D_RWKV + 2 * RW_LORA_W + 2 * RW_LORA_A + RW_LORA_G
GDN_COLS = 4 * D_GDN + 4 * GDN_HEADS

LANES = 128
SUBLANES = 8
MXU_WIDTH = 256
VMEM_LIMIT_BYTES = 56 * 1024 * 1024
GDN_COLS_PAD = 4 * D_GDN + LANES
CHUNK = 64
HEAD_DIM = 64
PAIR = 2 * HEAD_DIM
FFT_NB = LANES


def _cparams(*sem):
    return pltpu.CompilerParams(dimension_semantics=sem, vmem_limit_bytes=VMEM_LIMIT_BYTES)


def _full(shape):
    nd = len(shape)
    return pl.BlockSpec(shape, lambda *_: (0,) * nd)


NN = (((1,), (0,)), ((), ()))
NT = (((1,), (1,)), ((), ()))
TN = (((0,), (0,)), ((), ()))


def _dg(a, b, dims):
    return lax.dot_general(a, b, dims, preferred_element_type=F32)


def _split2(x):
    hi = x.astype(BF16)
    lo = (x - hi.astype(F32)).astype(BF16)
    return hi, lo


def _split3(x):
    hi = x.astype(BF16)
    r = x - hi.astype(F32)
    mid = r.astype(BF16)
    lo = (r - mid.astype(F32)).astype(BF16)
    return hi, mid, lo


def _mm(a, b, dims=NN, passes=1):
    if passes == 1:
        return _dg(a.astype(BF16), b.astype(BF16), dims)
    ah, al = _split2(a)
    bh, bl = _split2(b)
    return _dg(ah, bh, dims) + (_dg(al, bh, dims) + _dg(ah, bl, dims))


def _mm_lconst(c, x, dims=NN):
    cb = c.astype(BF16)
    h, m, l = _split3(x)
    return _dg(cb, h, dims) + (_dg(cb, m, dims) + _dg(cb, l, dims))


def _mm_rconst(x, c, dims=NN):
    cb = c.astype(BF16)
    h, m, l = _split3(x)
    return _dg(h, cb, dims) + (_dg(m, cb, dims) + _dg(l, cb, dims))


def _mm_hl(ch, cl, x, dims=NN):
    xh, xl = _split2(x)
    return _dg(ch, xh, dims) + (_dg(ch, xl, dims) + _dg(cl, xh, dims))


def _rms(x, g):
    return x * lax.rsqrt(jnp.mean(x * x, axis=-1, keepdims=True) + NORM_EPS) * g


def _sigmoid(x):
    return 1.0 / (1.0 + jnp.exp(-x))


def _silu(x):
    return x * _sigmoid(x)


def _softplus(x):
    return jnp.maximum(x, 0.0) + jnp.log(1.0 + jnp.exp(-jnp.abs(x)))


def _shift_rows(p, prev_row, next_row):
    n = p.shape[0]
    rows = lax.broadcasted_iota(jnp.int32, p.shape, 0)
    prev = jnp.where(rows == 0, prev_row, pltpu.roll(p, 1, 0))
    nxt = jnp.where(rows == n - 1, next_row, pltpu.roll(p, n - 1, 0))
    return prev, nxt


def _halo_specs(tm, width, L, col_block=0):
    r = tm // SUBLANES
    last = L // SUBLANES - 1

    def cur(b, i):
        return (b, i, col_block)

    def prev(b, i):
        return (b, jnp.maximum(i * r - 1, 0), col_block)

    def nxt(b, i):
        return (b, jnp.minimum((i + 1) * r, last), col_block)

    return (pl.BlockSpec((1, tm, width), cur),
            pl.BlockSpec((1, SUBLANES, width), prev),
            pl.BlockSpec((1, SUBLANES, width), nxt))


def _halo_rows(prev_ref, next_ref):
    i = pl.program_id(1)
    n = pl.num_programs(1)
    prev_row = jnp.where(i > 0, prev_ref[0, SUBLANES - 1:SUBLANES, :], 0.0)
    next_row = jnp.where(i < n - 1, next_ref[0, 0:1, :], 0.0)
    return prev_row, next_row


def _head_sum_matrix(width, head_dim, scale=1.0):
    idx = np.arange(width) // head_dim
    return jnp.asarray((idx[:, None] == idx[None, :]).astype(np.float32) * scale)


def _ffn_body(x_ref, g_ref, w1_ref, w3_ref, w2_ref, gf_ref, o_ref, acc_ref, *, n_chunks, tf, final):
    x = x_ref[...]
    h = _rms(x, g_ref[...]).astype(BF16)
    for j in range(n_chunks):
        sl = slice(j * tf, (j + 1) * tf)
        a = _dg(h, w1_ref[:, sl], NN)
        b = _dg(h, w3_ref[:, sl], NN)
        t = (_silu(a) * b).astype(BF16)
        part = _dg(t, w2_ref[sl, :], NN)
        if j == 0:
            acc_ref[...] = part
        else:
            acc_ref[...] += part
    y = x + 0.5 * acc_ref[...]
    if final:
        y = _rms(y, gf_ref[...])
    o_ref[...] = y


def _ffn(x, g, w1, w3, w2, gf, final):
    T, D = x.shape
    FF = w1.shape[1]
    tm = min(512, T)
    tf = MXU_WIDTH
    body = functools.partial(_ffn_body, n_chunks=FF // tf, tf=tf, final=final)
    return pl.pallas_call(
        body,
        out_shape=jax.ShapeDtypeStruct((T, D), F32),
        grid=(T // tm,),
        in_specs=[pl.BlockSpec((tm, D), lambda i: (i, 0)), _full((1, D)),
                  _full((D, FF)), _full((D, FF)), _full((FF, D)), _full((1, D))],
        out_specs=pl.BlockSpec((tm, D), lambda i: (i, 0)),
        scratch_shapes=[pltpu.VMEM((tm, D), F32)],
        compiler_params=_cparams("parallel"),
        name="ffn_final" if final else "ffn",
    )(x, g.reshape(1, D), w1, w3, w2, gf.reshape(1, D))


def _inproj_body(x_ref, g_ref, w_ref, ohy_ref, orw_ref, ogd_ref):
    h = _rms(x_ref[...], g_ref[...]).astype(BF16)
    c0, c1 = HY_COLS, HY_COLS + RW_COLS
    ohy_ref[...] = _dg(h, w_ref[:, 0:c0], NN)
    orw_ref[...] = _dg(h, w_ref[:, c0:c1], NN)
    ogd_ref[...] = _dg(h, w_ref[:, c1:c1 + GDN_COLS_PAD], NN)


def _inproj(x, g, w):
    T, D = x.shape
    tm = min(256, T)
    W = w.shape[1]
    widths = (HY_COLS, RW_COLS, GDN_COLS_PAD)
    return pl.pallas_call(
        _inproj_body,
        out_shape=tuple(jax.ShapeDtypeStruct((T, n), F32) for n in widths),
        grid=(T // tm,),
        in_specs=[pl.BlockSpec((tm, D), lambda i: (i, 0)), _full((1, D)), _full((D, W))],
        out_specs=tuple(pl.BlockSpec((tm, n), lambda i: (i, 0)) for n in widths),
        compiler_params=_cparams("parallel"),
        name="inproj",
    )(x, g.reshape(1, D), w)


def _hy_filter_body(z_ref, t_ref, freq_ref, w1_ref, b1_ref, w2_ref, b2_ref, w3_ref, dec_ref,
                    h_ref, s_ref):
    i = pl.program_id(0)
    freq = freq_ref[...]
    h = jnp.sin(freq * (_mm(z_ref[...], w1_ref[...], passes=3) + b1_ref[...]))
    h = jnp.sin(freq * (_mm(h, w2_ref[...], passes=3) + b2_ref[...]))
    h = _mm(h, w3_ref[...], passes=3) * jnp.exp(-t_ref[...] * dec_ref[...])
    C = D_HYENA
    fwd = h[:, :C]
    bwd = h[:, C:]
    rows = lax.broadcasted_iota(jnp.int32, bwd.shape, 0)
    bwd = jnp.where((rows == 0) & (i == 0), 0.0, bwd)
    h_ref[0] = fwd
    h_ref[1] = bwd
    part = jnp.sum(jnp.abs(fwd) + jnp.abs(bwd), axis=0, keepdims=True)

    @pl.when(i == 0)
    def _():
        s_ref[...] = part

    @pl.when(i > 0)
    def _():
        s_ref[...] += part


def _hy_filter(z, t, freq, w1, b1, w2, b2, w3, decay):
    L = z.shape[0]
    tl = min(1024, L)
    C = D_HYENA
    return pl.pallas_call(
        _hy_filter_body,
        out_shape=(jax.ShapeDtypeStruct((2, L, C), F32), jax.ShapeDtypeStruct((1, C), F32)),
        grid=(L // tl,),
        in_specs=[pl.BlockSpec((tl, LANES), lambda i: (i, 0)), pl.BlockSpec((tl, 1), lambda i: (i, 0)),
                  _full((1, HY_FFN)), _full((LANES, HY_FFN)), _full((1, HY_FFN)),
                  _full((HY_FFN, HY_FFN)), _full((1, HY_FFN)), _full((HY_FFN, 2 * C)), _full((1, 2 * C))],
        out_specs=(pl.BlockSpec((2, tl, C), lambda i: (0, i, 0)), _full((1, C))),
        compiler_params=_cparams("arbitrary"),
        name="hy_filter",
    )(z, t, freq.reshape(1, -1), w1, b1.reshape(1, -1), w2, b2.reshape(1, -1), w3, decay.reshape(1, -1))


def _hy_pre_body(p_ref, pp_ref, pn_ref, w_ref, b_ref, u_ref, x0_ref):
    p = p_ref[0]
    prev_row, next_row = _halo_rows(pp_ref, pn_ref)
    prev, nxt = _shift_rows(p, prev_row, next_row)
    c = w_ref[0:1, :] * prev + w_ref[1:2, :] * p + w_ref[2:3, :] * nxt + b_ref[...]
    C = D_HYENA
    x0_ref[0] = c[:, :C]
    u_ref[0] = c[:, C:2 * C] * c[:, 2 * C:]


def _hy_pre(p, w, b):
    B, L, W = p.shape
    tm = min(512, L)
    C = D_HYENA
    return pl.pallas_call(
        _hy_pre_body,
        out_shape=(jax.ShapeDtypeStruct((B, L, C), F32),) * 2,
        grid=(B, L // tm),
        in_specs=[*_halo_specs(tm, W, L), _full((3, W)), _full((1, W))],
        out_specs=(pl.BlockSpec((1, tm, C), lambda b, i: (b, i, 0)),) * 2,
        compiler_params=_cparams("parallel", "parallel"),
        name="hy_pre",
    )(p, p, p, w, b.reshape(1, W))


def _fft_a_body(u_ref, mh_ref, ml_ref, twr_ref, twi_ref, o_ref, *, na):
    a = _mm_hl(mh_ref[...], ml_ref[...], u_ref[0])
    ar, ai = a[:na], a[na:]
    twr, twi = twr_ref[...], twi_ref[...]
    o_ref[0, 0] = ar * twr - ai * twi
    o_ref[0, 1] = ar * twi + ai * twr


def _fft_a(u2d, mh, ml, twr, twi):
    B, half, cols = u2d.shape
    na = 2 * half
    tn = min(2048, cols)
    return pl.pallas_call(
        functools.partial(_fft_a_body, na=na),
        out_shape=jax.ShapeDtypeStruct((B, 2, na, cols), F32),
        grid=(B, cols // tn),
        in_specs=[pl.BlockSpec((1, half, tn), lambda b, j: (b, 0, j)),
                  _full((2 * na, half)), _full((2 * na, half)),
                  pl.BlockSpec((na, tn), lambda b, j: (0, j)), pl.BlockSpec((na, tn), lambda b, j: (0, j))],
        out_specs=pl.BlockSpec((1, 2, na, tn), lambda b, j: (b, 0, 0, j)),
        compiler_params=_cparams("parallel", "parallel"),
        name="fft_a",
    )(u2d, mh, ml, twr, twi)


def _fft_filt_body(bf_ref, bb_ref, mh_ref, ml_ref, s_ref, o_ref, *, tk):
    nb = FFT_NB
    inv = 1.0 / s_ref[...]
    for k in range(tk):
        xf = _mm_hl(mh_ref[...], ml_ref[...], jnp.concatenate([bf_ref[0, 0, k], bf_ref[0, 1, k]], axis=0))
        xb = _mm_hl(mh_ref[...], ml_ref[...], jnp.concatenate([bb_ref[0, 0, k], bb_ref[0, 1, k]], axis=0))
        o_ref[0, k] = (xf[:nb] + xb[:nb]) * inv
        o_ref[1, k] = (xf[nb:] - xb[nb:]) * inv


def _fft_filt(bt, mh, ml, s):
    _, _, na, nb, C = bt.shape
    tk = min(4, na)
    return pl.pallas_call(
        functools.partial(_fft_filt_body, tk=tk),
        out_shape=jax.ShapeDtypeStruct((2, na, nb, C), F32),
        grid=(na // tk,),
        in_specs=[pl.BlockSpec((1, 2, tk, nb, C), lambda i: (0, 0, i, 0, 0)),
                  pl.BlockSpec((1, 2, tk, nb, C), lambda i: (1, 0, i, 0, 0)),
                  _full((2 * nb, 2 * nb)), _full((2 * nb, 2 * nb)), _full((1, C))],
        out_specs=pl.BlockSpec((2, tk, nb, C), lambda i: (0, i, 0, 0)),
        compiler_params=_cparams("parallel"),
        name="fft_filt",
    )(bt, bt, mh, ml, s)


def _fft_c_body(b_ref, k_ref, mfh_ref, mfl_ref, mih_ref, mil_ref, twr_ref, twi_ref, o_ref, *, tk):
    nb = FFT_NB
    for k in range(tk):
        x = _mm_hl(mfh_ref[...], mfl_ref[...], jnp.concatenate([b_ref[0, 0, k], b_ref[0, 1, k]], axis=0))
        xr, xi = x[:nb], x[nb:]
        kr, ki = k_ref[0, k], k_ref[1, k]
        y = jnp.concatenate([xr * kr - xi * ki, xr * ki + xi * kr], axis=0)
        d = _mm_hl(mih_ref[...], mil_ref[...], y)
        dr, di = d[:nb], d[nb:]
        reps = dr.shape[1] // LANES
        twr = jnp.concatenate([twr_ref[k]] * reps, axis=1)
        twi = jnp.concatenate([twi_ref[k]] * reps, axis=1)
        o_ref[0, 0, k] = dr * twr + di * twi
        o_ref[0, 1, k] = di * twr - dr * twi


def _fft_c(bt, khat, mfh, mfl, mih, mil, twr, twi):
    B, _, na, nb, C = bt.shape
    tk = min(4, na)
    blk = pl.BlockSpec((1, 2, tk, nb, C), lambda b, i: (b, 0, i, 0, 0))
    m = _full((2 * nb, 2 * nb))
    tw = pl.BlockSpec((tk, nb, LANES), lambda b, i: (i, 0, 0))
    return pl.pallas_call(
        functools.partial(_fft_c_body, tk=tk),
        out_shape=jax.ShapeDtypeStruct(bt.shape, F32),
        grid=(B, na // tk),
        in_specs=[blk, pl.BlockSpec((2, tk, nb, C), lambda b, i: (0, i, 0, 0)), m, m, m, m, tw, tw],
        out_specs=blk,
        compiler_params=_cparams("parallel", "parallel"),
        name="fft_c",
    )(bt, khat, mfh, mfl, mih, mil, twr, twi)


def _fft_out_body(e_ref, mh_ref, ml_ref, u_ref, x0_ref, bias_ref, o_ref):
    conv = _mm_hl(mh_ref[...], ml_ref[...], e_ref[0])
    u = u_ref[0]
    o_ref[0] = x0_ref[0] * (conv + bias_ref[...] * u)


def _fft_out(e2d, mh, ml, u2d, x02d, bias_t):
    B, half, cols = u2d.shape
    tn = min(2048, cols)
    blk = pl.BlockSpec((1, half, tn), lambda b, j: (b, 0, j))
    return pl.pallas_call(
        _fft_out_body,
        out_shape=jax.ShapeDtypeStruct(u2d.shape, F32),
        grid=(B, cols // tn),
        in_specs=[pl.BlockSpec((1, 4 * half, tn), lambda b, j: (b, 0, j)),
                  _full((half, 4 * half)), _full((half, 4 * half)), blk, blk,
                  pl.BlockSpec((1, tn), lambda b, j: (0, j))],
        out_specs=blk,
        compiler_params=_cparams("parallel", "parallel"),
        name="fft_out",
    )(e2d, mh, ml, u2d, x02d, bias_t)


def _hl(m):
    m = jnp.asarray(m, F32)
    hi = m.astype(BF16)
    return hi, (m - hi.astype(F32)).astype(BF16)


@functools.lru_cache(maxsize=None)
def _fft_consts_np(L):
    n = 2 * L
    nb = FFT_NB
    na = n // nb
    half = na // 2
    ka = np.arange(na)
    fa = np.exp(-2j * np.pi * np.outer(ka, ka) / na)
    kb = np.arange(nb)
    fb = np.exp(-2j * np.pi * np.outer(kb, kb) / nb)
    m_a = np.concatenate([fa.real[:, :half], fa.imag[:, :half]], axis=0)
    m_f = np.block([[fb.real, -fb.imag], [fb.imag, fb.real]])
    m_i = np.block([[fb.real, fb.imag], [-fb.imag, fb.real]])
    m_o = np.concatenate([fa.real[:half, :], fa.imag[:half, :]], axis=1) / n
    tw = np.exp(-2j * np.pi * (np.outer(ka, kb) % n) / n)
    f32 = lambda a: np.asarray(a, np.float32)
    return f32(m_a), f32(m_f), f32(m_i), f32(m_o), f32(tw.real), f32(tw.imag)


def _hyena_pos(L):
    t = jnp.linspace(0.0, 1.0, L, dtype=F32)[:, None]
    ang = 2.0 * math.pi * jnp.arange(L, dtype=F32)[:, None] / L
    bands = jnp.linspace(1e-4, HY_BANDS - 1, HY_BANDS, dtype=F32)[None, :]
    z = jnp.concatenate([t, jnp.cos(bands * ang), -jnp.sin(bands * ang)], axis=-1)
    return jnp.pad(z, ((0, 0), (0, LANES - HY_EMB))), t


def _hyena(p_hy, z, t, prm, l):
    B, L, _ = p_hy.shape
    C = D_HYENA
    nb = FFT_NB
    na = 2 * L // nb
    half = na // 2
    m_a, m_f, m_i, m_o, twr, twi = _fft_consts_np(L)
    mah, mal = _hl(m_a)
    mfh, mfl = _hl(m_f)
    mih, mil = _hl(m_i)
    moh, mol = _hl(m_o)
    twr_cols = jnp.repeat(jnp.asarray(twr), C, axis=1)
    twi_cols = jnp.repeat(jnp.asarray(twi), C, axis=1)
    twr_l = jnp.broadcast_to(jnp.asarray(twr)[:, :, None], (na, nb, LANES))
    twi_l = jnp.broadcast_to(jnp.asarray(twi)[:, :, None], (na, nb, LANES))

    w1 = jnp.pad(prm["hy_w1"][l], ((0, LANES - HY_EMB), (0, 0)))
    hraw, hsum = _hy_filter(z, t, prm["hy_freq"][l], w1, prm["hy_b1"][l], prm["hy_w2"][l],
                            prm["hy_b2"][l], prm["hy_w3"][l], prm["hy_decay"][l])
    hb = _fft_a(hraw.reshape(2, half, nb * C), mah, mal, twr_cols, twi_cols)
    khat = _fft_filt(hb.reshape(2, 2, na, nb, C), mfh, mfl, hsum)

    u, x0 = _hy_pre(p_hy, prm["hy_conv_w"][l], prm["hy_conv_b"][l])
    u2d = u.reshape(B, half, nb * C)
    bt = _fft_a(u2d, mah, mal, twr_cols, twi_cols)
    e = _fft_c(bt.reshape(B, 2, na, nb, C), khat, mfh, mfl, mih, mil, twr_l, twi_l)
    bias_t = jnp.tile(prm["hy_bias"][l], nb).reshape(1, nb * C)
    y = _fft_out(e.reshape(B, 2 * na, nb * C), moh, mol, u2d, x0.reshape(B, half, nb * C), bias_t)
    return y.reshape(B * L, C)


def _pair_masks():
    lane = lax.broadcasted_iota(jnp.int32, (1, PAIR), 1)
    return (lane < HEAD_DIM).astype(F32), (lane >= HEAD_DIM).astype(F32)


def _sm(x, m0, m1):
    return jnp.concatenate([x * m0, x * m1], axis=0)


def _tri_masks(reverse):
    n = 2 * CHUNK
    r = lax.broadcasted_iota(jnp.int32, (n, n), 0)
    c = lax.broadcasted_iota(jnp.int32, (n, n), 1)
    same = (r // CHUNK) == (c // CHUNK)
    if reverse:
        return same & (c > r), same & (c >= r)
    return same & (c < r), same & (c <= r)


def _cum_matrix(reverse):
    r = lax.broadcasted_iota(jnp.int32, (CHUNK, CHUNK), 0)
    c = lax.broadcasted_iota(jnp.int32, (CHUNK, CHUNK), 1)
    return ((c >= r) if reverse else (c <= r)).astype(F32)


NEUMANN_PASSES = 3
GRAM_PASSES = 3


def _neumann_inverse(a):
    n = a.shape[0]
    r = lax.broadcasted_iota(jnp.int32, (n, n), 0)
    c = lax.broadcasted_iota(jnp.int32, (n, n), 1)
    t = jnp.where(r == c, 1.0, 0.0) + a
    p = a
    steps = int(math.log2(CHUNK)) - 1
    for _ in range(steps):
        p = _mm(p, p, passes=NEUMANN_PASSES)
        t = t + _mm(t, p, passes=NEUMANN_PASSES)
    return t


def _rw_prep_body(p_ref, pp_ref, pn_ref, mup_ref, mun_ref, wl_ref, w0_ref, al_ref, a0_ref, gl_ref,
                  kk_ref, ka_ref, rk_ref, hs_ref,
                  r_o, v_o, kkn_o, gate_o, bonus_o, lwf_o, kdf_o, af_o, lwb_o, kdb_o, ab_o):
    p = p_ref[0]
    prev_row, next_row = _halo_rows(pp_ref, pn_ref)
    prev, nxt = _shift_rows(p, prev_row, next_row)
    p = p + mup_ref[...] * (prev - p) + mun_ref[...] * (nxt - p)
    D = D_RWKV
    r, k, v = p[:, 0:D], p[:, D:2 * D], p[:, 2 * D:3 * D]
    o = 3 * D
    lw = (p[:, o:o + RW_LORA_W], p[:, o + RW_LORA_W:o + 2 * RW_LORA_W])
    o += 2 * RW_LORA_W
    la = (p[:, o:o + RW_LORA_A], p[:, o + RW_LORA_A:o + 2 * RW_LORA_A])
    o += 2 * RW_LORA_A
    lg = p[:, o:o + RW_LORA_G]
    hs = hs_ref[...]
    gate_o[0] = _mm(_sigmoid(lg), gl_ref[...], passes=3)
    k2 = k * kk_ref[...]
    kkn = k2 * lax.rsqrt(_mm_rconst(k2 * k2, hs) + L2_EPS)
    r_o[0] = r
    v_o[0] = v
    kkn_o[0] = kkn
    bonus_o[0] = _mm_rconst(r * k * rk_ref[...], hs) * v
    outs = ((lwf_o, kdf_o, af_o), (lwb_o, kdb_o, ab_o))
    for d in range(2):
        lw_o, kd_o, a_o = outs[d]
        logw = -RW_DECAY_SCALE * _sigmoid(w0_ref[d:d + 1, :] + _mm(jnp.tanh(lw[d]), wl_ref[d], passes=3))
        a = _sigmoid(a0_ref[d:d + 1, :] + _mm(la[d], al_ref[d], passes=3))
        lw_o[0] = logw
        kd_o[0] = k * (1.0 + (a - 1.0) * ka_ref[...])
        a_o[0] = a


def _rw_prep(p, prm, l):
    B, L, W = p.shape
    tm = min(256, L)
    D = D_RWKV
    row = lambda a: a.reshape(1, -1)
    out = pl.BlockSpec((1, tm, D), lambda b, i: (b, i, 0))
    return pl.pallas_call(
        _rw_prep_body,
        out_shape=(jax.ShapeDtypeStruct((B, L, D), F32),) * 11,
        grid=(B, L // tm),
        in_specs=[*_halo_specs(tm, W, L), _full((1, W)), _full((1, W)),
                  _full((2, RW_LORA_W, D)), _full((2, D)), _full((2, RW_LORA_A, D)), _full((2, D)),
                  _full((RW_LORA_G, D)), _full((1, D)), _full((1, D)), _full((1, D)), _full((D, D))],
        out_specs=(out,) * 11,
        compiler_params=_cparams("parallel", "parallel"),
        name="rw_prep",
    )(p, p, p, row(prm["rw_mu_prev"][l]), row(prm["rw_mu_next"][l]), prm["rw_w_lora"][l], prm["rw_w0"][l],
      prm["rw_a_lora"][l], prm["rw_a0"][l], prm["rw_g_lora"][l], row(prm["rw_k_k"][l]),
      row(prm["rw_k_a"][l]), row(prm["rw_r_k"][l]), _head_sum_matrix(D, RW_HEAD_DIM))


def _rw_chunk(r, v, kk, lw, kd, a, s0, reverse, m0, m1):
    C = CHUNK
    cum = _mm_lconst(_cum_matrix(reverse), lw)
    pin = jnp.exp(cum)
    at = _sm(-kk * jnp.exp(cum - lw), m0, m1)
    rt = _sm(r * pin, m0, m1)
    pinv = jnp.exp(-cum)
    kh = _sm(kd * pinv, m0, m1)
    bh = _sm(kk * a * pinv, m0, m1)
    vs = _sm(v, m0, m1)
    tot = cum[0:1, :] if reverse else cum[C - 1:C, :]
    strict, incl = _tri_masks(reverse)
    g = _mm(jnp.concatenate([at, rt], axis=0), jnp.concatenate([kh, bh], axis=0), NT, passes=GRAM_PASSES)
    n = 2 * C
    a_ak = jnp.where(strict, g[:n, :n], 0.0)
    a_ab = jnp.where(strict, g[:n, n:], 0.0)
    a_rk = jnp.where(incl, g[n:, :n], 0.0)
    a_rb = jnp.where(incl, g[n:, n:], 0.0)
    t = _neumann_inverse(a_ab)
    av = _mm(jnp.concatenate([a_ak, a_rk], axis=0), vs)
    uw = _mm(t, jnp.concatenate([av[:n], at], axis=1))
    ws = _mm(jnp.concatenate([uw[:, PAIR:], rt], axis=0), s0, NT)
    u = uw[:, :PAIR] + ws[:n]
    y = ws[n:] + av[n:] + _mm(a_rb, u)
    upd = _mm(jnp.concatenate([vs, u], axis=0), jnp.concatenate([kh, bh], axis=0), TN)
    s_new = (s0 + upd) * jnp.exp(tot)
    return y[:C] + y[C:], s_new


def _rw_scan_body(rf, vf, kkf, lwf, kdf, af, rb, vb, kkb, lwb, kdb, ab, yf_o, yb_o, s_ref):
    @pl.when(pl.program_id(1) == 0)
    def _():
        s_ref[...] = jnp.zeros(s_ref.shape, F32)

    m0, m1 = _pair_masks()
    dirs = ((rf, vf, kkf, lwf, kdf, af, yf_o, False), (rb, vb, kkb, lwb, kdb, ab, yb_o, True))
    for d, (r, v, kk, lw, kd, a, y_o, rev) in enumerate(dirs):
        for j in range(D_RWKV // PAIR):
            sl = slice(j * PAIR, (j + 1) * PAIR)
            y, s_new = _rw_chunk(r[0, :, sl], v[0, :, sl], kk[0, :, sl], lw[0, :, sl], kd[0, :, sl],
                                 a[0, :, sl], s_ref[d, j], rev, m0, m1)
            y_o[0, :, sl] = y
            s_ref[d, j] = s_new


def _rw_scan(r, v, kk, lwf, kdf, af, lwb, kdb, ab):
    B, L, D = r.shape
    n = L // CHUNK
    fw = pl.BlockSpec((1, CHUNK, D), lambda b, i: (b, i, 0))
    bw = pl.BlockSpec((1, CHUNK, D), lambda b, i: (b, n - 1 - i, 0))
    return pl.pallas_call(
        _rw_scan_body,
        out_shape=(jax.ShapeDtypeStruct((B, L, D), F32),) * 2,
        grid=(B, n),
        in_specs=[fw] * 6 + [bw] * 6,
        out_specs=(fw, bw),
        scratch_shapes=[pltpu.VMEM((2, D // PAIR, PAIR, PAIR), F32)],
        compiler_params=_cparams("parallel", "arbitrary"),
        name="rw_scan",
    )(r, v, kk, lwf, kdf, af, r, v, kk, lwb, kdb, ab)


GDN_G_LANE = 0
GDN_BETA_LANE = 2 * GDN_HEADS


def _gdn_prep_body(p_ref, pp_ref, pn_ref, s_ref, w_ref, alog_ref, dt_ref, hs_ref, q_o, k_o, v_o, gs_o):
    p = p_ref[0]
    prev_row, next_row = _halo_rows(pp_ref, pn_ref)
    prev, nxt = _shift_rows(p, prev_row, next_row)
    c = _silu(w_ref[0:1, :] * prev + w_ref[1:2, :] * p + w_ref[2:3, :] * nxt)
    D = D_GDN
    q, k, v = c[:, 0:D], c[:, D:2 * D], c[:, 2 * D:3 * D]
    hs = hs_ref[...]
    q_o[0] = q * lax.rsqrt(_mm_rconst(q * q, hs) + L2_EPS) * (GDN_HEAD_DIM ** -0.5)
    k_o[0] = k * lax.rsqrt(_mm_rconst(k * k, hs) + L2_EPS)
    v_o[0] = v
    s = s_ref[0]
    lane = lax.broadcasted_iota(jnp.int32, s.shape, 1)
    g = -jnp.exp(alog_ref[...]) * _softplus(s + dt_ref[...])
    gs_o[0] = jnp.where(lane < GDN_BETA_LANE, g, jnp.where(lane < 4 * GDN_HEADS, _sigmoid(s), 0.0))


def _gdn_prep(p, prm, l):
    B, L, W = p.shape
    tm = min(256, L)
    D = D_GDN
    Wq = 3 * D
    pad = LANES - 2 * GDN_HEADS
    alog = jnp.pad(prm["gdn_a_log"][l].reshape(-1), (0, pad)).reshape(1, LANES)
    dt = jnp.pad(prm["gdn_dt_bias"][l].reshape(-1), (0, pad)).reshape(1, LANES)
    out = pl.BlockSpec((1, tm, D), lambda b, i: (b, i, 0))
    side = pl.BlockSpec((1, tm, LANES), lambda b, i: (b, i, 4 * D // LANES))
    return pl.pallas_call(
        _gdn_prep_body,
        out_shape=(jax.ShapeDtypeStruct((B, L, D), F32),) * 3 + (jax.ShapeDtypeStruct((B, L, LANES), F32),),
        grid=(B, L // tm),
        in_specs=[*_halo_specs(tm, Wq, L), side, _full((3, Wq)), _full((1, LANES)), _full((1, LANES)),
                  _full((D, D))],
        out_specs=(out,) * 3 + (pl.BlockSpec((1, tm, LANES), lambda b, i: (b, i, 0)),),
        compiler_params=_cparams("parallel", "parallel"),
        name="gdn_prep",
    )(p, p, p, p, prm["gdn_conv_w"][l], alog, dt, _head_sum_matrix(D, GDN_HEAD_DIM))


def _col_pair(x, lane0):
    return jnp.concatenate([x[:, lane0:lane0 + 1], x[:, lane0 + 1:lane0 + 2]], axis=0)


def _gdn_chunk(q, k, v, gs, gcum, d, j, s0, reverse, m0, m1):
    C = CHUNK
    n = 2 * C
    h0 = 2 * j
    g_lane = GDN_G_LANE + d * GDN_HEADS + h0
    b_lane = GDN_BETA_LANE + d * GDN_HEADS + h0
    gc = _col_pair(gcum, g_lane)
    beta = _col_pair(gs, b_lane)
    lane = lax.broadcasted_iota(jnp.int32, (1, LANES), 1)
    sel = jnp.concatenate([jnp.broadcast_to((lane == g_lane).astype(F32), (C, LANES)),
                           jnp.broadcast_to((lane == g_lane + 1).astype(F32), (C, LANES))], axis=0)
    gq = jnp.concatenate([gcum, gcum], axis=0) * sel
    ones = jnp.ones((n, LANES), F32)
    gcol = _mm_rconst(gq, ones, NT)
    grow = _mm_lconst(ones, gq, NT)
    strict, incl = _tri_masks(reverse)
    dec = jnp.exp(jnp.where(incl, gcol - grow, -1e30))
    ks = _sm(k, m0, m1)
    qs = _sm(q, m0, m1)
    vs = _sm(v, m0, m1)
    kb = ks * beta
    kq = _mm(jnp.concatenate([kb, qs], axis=0), ks, NT, passes=GRAM_PASSES)
    a = jnp.where(strict, kq[:n] * dec, 0.0)
    attn = kq[n:] * dec
    t = _neumann_inverse(-a)
    egc = jnp.exp(gc)
    sol = _mm(t, jnp.concatenate([vs * beta, kb * egc], axis=1))
    ws = _mm(jnp.concatenate([sol[:, PAIR:], qs * egc], axis=0), s0)
    v_new = sol[:, :PAIR] - ws[:n]
    o = ws[n:] + _mm(attn, v_new)
    glast = gc[0:1] if reverse else gc[C - 1:C]
    glast1 = gc[C:C + 1] if reverse else gc[n - 1:n]
    rows = lax.broadcasted_iota(jnp.int32, (n, 1), 0)
    gl = jnp.where(rows < C, glast, glast1)
    upd = _mm(ks * jnp.exp(gl - gc), v_new, TN)
    rows_s = lax.broadcasted_iota(jnp.int32, (PAIR, 1), 0)
    s_new = s0 * jnp.exp(jnp.where(rows_s < HEAD_DIM, glast, glast1)) + upd
    return o[:C] + o[C:], s_new


def _gdn_scan_body(qf, kf, vf, gf, qb, kb, vb, gb, of_o, ob_o, s_ref):
    @pl.when(pl.program_id(1) == 0)
    def _():
        s_ref[...] = jnp.zeros(s_ref.shape, F32)

    m0, m1 = _pair_masks()
    dirs = ((qf, kf, vf, gf, of_o, False), (qb, kb, vb, gb, ob_o, True))
    for d, (q, k, v, g, o_o, rev) in enumerate(dirs):
        gs = g[0]
        gcum = _mm_lconst(_cum_matrix(rev), gs)
        for j in range(D_GDN // PAIR):
            sl = slice(j * PAIR, (j + 1) * PAIR)
            o, s_new = _gdn_chunk(q[0, :, sl], k[0, :, sl], v[0, :, sl], gs, gcum, d, j, s_ref[d, j],
                                  rev, m0, m1)
            o_o[0, :, sl] = o
            s_ref[d, j] = s_new


def _gdn_scan(q, k, v, gs):
    B, L, D = q.shape
    n = L // CHUNK
    fw = pl.BlockSpec((1, CHUNK, D), lambda b, i: (b, i, 0))
    bw = pl.BlockSpec((1, CHUNK, D), lambda b, i: (b, n - 1 - i, 0))
    fws = pl.BlockSpec((1, CHUNK, LANES), lambda b, i: (b, i, 0))
    bws = pl.BlockSpec((1, CHUNK, LANES), lambda b, i: (b, n - 1 - i, 0))
    return pl.pallas_call(
        _gdn_scan_body,
        out_shape=(jax.ShapeDtypeStruct((B, L, D), F32),) * 2,
        grid=(B, n),
        in_specs=[fw, fw, fw, fws, bw, bw, bw, bws],
        out_specs=(fw, bw),
        scratch_shapes=[pltpu.VMEM((2, D // PAIR, PAIR, PAIR), F32)],
        compiler_params=_cparams("parallel", "arbitrary"),
        name="gdn_scan",
    )(q, k, v, gs, q, k, v, gs)


def _mix_out_body(x_ref, yhy_ref, yf_ref, yb_ref, gate_ref, bonus_ref, of_ref, ob_ref, zg_ref,
                  gnw_ref, gnb_ref, nw_ref, avg_ref, w_ref, o_ref):
    avg = avg_ref[...]
    y = yf_ref[...] + yb_ref[...]
    mu = _mm_rconst(y, avg)
    dlt = y - mu
    var = _mm_rconst(dlt * dlt, avg)
    y_rw = (dlt * lax.rsqrt(var + RW_GN_EPS) * gnw_ref[...] + gnb_ref[...] + bonus_ref[...]) * gate_ref[...]
    o = of_ref[...] + ob_ref[...]
    ms = _mm_rconst(o * o, avg)
    y_gdn = o * lax.rsqrt(ms + NORM_EPS) * nw_ref[...] * _silu(zg_ref[...])
    c0, c1 = D_HYENA, D_HYENA + D_RWKV
    acc = _dg(yhy_ref[...].astype(BF16), w_ref[0:c0, :], NN)
    acc += _dg(y_rw.astype(BF16), w_ref[c0:c1, :], NN)
    acc += _dg(y_gdn.astype(BF16), w_ref[c1:, :], NN)
    o_ref[...] = x_ref[...] + acc


def _mix_out(x, y_hy, yf, yb, gate, bonus, of, ob, p_gdn, prm, l, w_out):
    T, D = x.shape
    tm = min(512, T)
    Dh = D_RWKV
    row = lambda a: a.reshape(1, -1)
    tile = lambda n: pl.BlockSpec((tm, n), lambda i: (i, 0))
    zg = pl.BlockSpec((tm, D_GDN), lambda i: (i, 3))
    avg = _head_sum_matrix(Dh, HEAD_DIM, 1.0 / HEAD_DIM)
    return pl.pallas_call(
        _mix_out_body,
        out_shape=jax.ShapeDtypeStruct((T, D), F32),
        grid=(T // tm,),
        in_specs=[tile(D), tile(D_HYENA)] + [tile(Dh)] * 6 + [zg, _full((1, Dh)), _full((1, Dh)),
                  _full((1, Dh)), _full((Dh, Dh)), _full((D, D))],
        out_specs=tile(D),
        compiler_params=_cparams("parallel"),
        name="mix_out",
    )(x, y_hy, yf, yb, gate, bonus, of, ob, p_gdn, row(prm["rw_gn_w"][l]), row(prm["rw_gn_b"][l]),
      row(jnp.tile(prm["gdn_norm_w"][l], GDN_HEADS)), avg, w_out)


def _mem_kv_body(m_ref, g_ref, wk_ref, wv_ref, k_o, v_o):
    h = _rms(m_ref[0], g_ref[...]).astype(BF16)
    k_o[0] = _dg(h, wk_ref[...], NN).astype(BF16)
    v_o[0] = _dg(h, wv_ref[...], NN).astype(BF16)


def _mem_kv(mem, g, wk, wv):
    B, M, D = mem.shape
    blk = pl.BlockSpec((1, M, D), lambda b: (b, 0, 0))
    return pl.pallas_call(
        _mem_kv_body,
        out_shape=(jax.ShapeDtypeStruct((B, M, D), BF16),) * 2,
        grid=(B,),
        in_specs=[blk, _full((1, D)), _full((D, D)), _full((D, D))],
        out_specs=(blk, blk),
        compiler_params=_cparams("parallel"),
        name="mem_kv",
    )(mem, g.reshape(1, D), wk, wv)


def _xattn_body(x_ref, g_ref, wq_ref, k_ref, v_ref, wo_ref, o_ref):
    x = x_ref[0]
    h = _rms(x, g_ref[...]).astype(BF16)
    q = (_dg(h, wq_ref[...], NN) * (XA_HEAD_DIM ** -0.5)).astype(BF16)
    outs = []
    for hd in range(XA_HEADS):
        sl = slice(hd * XA_HEAD_DIM, (hd + 1) * XA_HEAD_DIM)
        s = _dg(q[:, sl], k_ref[0, :, sl], NT)
        s = s - jnp.max(s, axis=-1, keepdims=True)
        e = jnp.exp(s)
        pr = e / jnp.sum(e, axis=-1, keepdims=True)
        outs.append(_dg(pr.astype(BF16), v_ref[0, :, sl], NN))
    o = jnp.concatenate(outs, axis=1).astype(BF16)
    o_ref[0] = x + _dg(o, wo_ref[...], NN)


def _xattn(x, g, wq, k, v, wo):
    B, L, D = x.shape
    M = k.shape[1]
    tm = min(512, L)
    tile = pl.BlockSpec((1, tm, D), lambda b, i: (b, i, 0))
    kv = pl.BlockSpec((1, M, D), lambda b, i: (b, 0, 0))
    return pl.pallas_call(
        _xattn_body,
        out_shape=jax.ShapeDtypeStruct((B, L, D), F32),
        grid=(B, L // tm),
        in_specs=[tile, _full((1, D)), _full((D, D)), kv, kv, _full((D, D))],
        out_specs=tile,
        compiler_params=_cparams("parallel", "parallel"),
        name="xattn",
    )(x, g.reshape(1, D), wq, k, v, wo)


def kernel(x, mem, norm_ffn1, ffn1_w1, ffn1_w3, ffn1_w2, norm_mix, w_in, w_out, hy_conv_w, hy_conv_b, hy_freq, hy_w1, hy_b1, hy_w2, hy_b2, hy_w3, hy_decay, hy_bias, rw_mu_prev, rw_mu_next, rw_w_lora, rw_w0, rw_a_lora, rw_a0, rw_g_lora, rw_k_k, rw_k_a, rw_r_k, rw_gn_w, rw_gn_b, gdn_conv_w, gdn_a_log, gdn_dt_bias, gdn_norm_w, norm_xattn, xa_wq, xa_wk, xa_wv, xa_wo, mem_norm, norm_ffn2, ffn2_w1, ffn2_w3, ffn2_w2, norm_final):
    prm = dict(hy_conv_w=hy_conv_w, hy_conv_b=hy_conv_b, hy_freq=hy_freq, hy_w1=hy_w1, hy_b1=hy_b1,
               hy_w2=hy_w2, hy_b2=hy_b2, hy_w3=hy_w3, hy_decay=hy_decay, hy_bias=hy_bias,
               rw_mu_prev=rw_mu_prev, rw_mu_next=rw_mu_next, rw_w_lora=rw_w_lora, rw_w0=rw_w0,
               rw_a_lora=rw_a_lora, rw_a0=rw_a0, rw_g_lora=rw_g_lora, rw_k_k=rw_k_k, rw_k_a=rw_k_a,
               rw_r_k=rw_r_k, rw_gn_w=rw_gn_w, rw_gn_b=rw_gn_b, gdn_conv_w=gdn_conv_w,
               gdn_a_log=gdn_a_log, gdn_dt_bias=gdn_dt_bias, gdn_norm_w=gdn_norm_w)
    B, L, D = x.shape
    depth = norm_ffn1.shape[0]
    T = B * L
    z_pos, t_pos = _hyena_pos(L)
    bf = lambda w: w.astype(BF16)
    w_in_p = jnp.pad(w_in, ((0, 0), (0, 0), (0, GDN_COLS_PAD - GDN_COLS)))
    xt = x.reshape(T, D)
    for l in range(depth):
        xt = _ffn(xt, norm_ffn1[l], bf(ffn1_w1[l]), bf(ffn1_w3[l]), bf(ffn1_w2[l]), norm_final, False)
        p_hy, p_rw, p_gdn = _inproj(xt, norm_mix[l], bf(w_in_p[l]))
        y_hy = _hyena(p_hy.reshape(B, L, -1), z_pos, t_pos, prm, l)
        r, v, kk, gate, bonus, lwf, kdf, af, lwb, kdb, ab = _rw_prep(p_rw.reshape(B, L, -1), prm, l)
        yf, yb = _rw_scan(r, v, kk, lwf, kdf, af, lwb, kdb, ab)
        q, k, vg, gs = _gdn_prep(p_gdn.reshape(B, L, -1), prm, l)
        of, ob = _gdn_scan(q, k, vg, gs)
        flat = lambda a: a.reshape(T, -1)
        xt = _mix_out(xt, y_hy, flat(yf), flat(yb), flat(gate), flat(bonus), flat(of), flat(ob), p_gdn,
                      prm, l, bf(w_out[l]))
        km, vm = _mem_kv(mem, mem_norm, bf(xa_wk[l]), bf(xa_wv[l]))
        xt = _xattn(xt.reshape(B, L, D), norm_xattn[l], bf(xa_wq[l]), km, vm, bf(xa_wo[l])).reshape(T, D)
        xt = _ffn(xt, norm_ffn2[l], bf(ffn2_w1[l]), bf(ffn2_w3[l]), bf(ffn2_w2[l]), norm_final,
                  l == depth - 1)
    return xt.reshape(B, L, D)
```

```python
import functools
import math

import numpy as np
import jax
import jax.numpy as jnp
from jax import lax
from jax.experimental import pallas as pl
from jax.experimental.pallas import tpu as pltpu

F32 = jnp.float32
BF16 = jnp.bfloat16

D_MODEL = 1024
D_HYENA = 256
RW_HEADS = 6
RW_HEAD_DIM = 64
D_RWKV = RW_HEADS * RW_HEAD_DIM
GDN_HEADS = 6
GDN_HEAD_DIM = 64
D_GDN = GDN_HEADS * GDN_HEAD_DIM
HY_BANDS = 16
HY_EMB = 1 + 2 * HY_BANDS
HY_FFN = 64
RW_LORA_W = 64
RW_LORA_A = 64
RW_LORA_G = 128
RW_DECAY_SCALE = 0.606531
RW_GN_EPS = 64e-5
XA_HEADS = 4
XA_HEAD_DIM = D_MODEL // XA_HEADS
D_FF = 2816
NORM_EPS = 1e-6
L2_EPS = 1e-6
HY_COLS = 3 * D_HYENA
RW_COLS = 3 * D_RWKV + 2 * RW_LORA_W + 2 * RW_LORA_A + RW_LORA_G
GDN_COLS = 4 * D_GDN + 4 * GDN_HEADS

LANES = 128
SUBLANES = 8
MXU_WIDTH = 256
VMEM_LIMIT_BYTES = 56 * 1024 * 1024
GDN_COLS_PAD = 4 * D_GDN + LANES
CHUNK = 64
HEAD_DIM = 64
PAIR = 2 * HEAD_DIM
FFT_NB = LANES


def _cparams(*sem):
    return pltpu.CompilerParams(dimension_semantics=sem, vmem_limit_bytes=VMEM_LIMIT_BYTES)


def _full(shape):
    nd = len(shape)
    return pl.BlockSpec(shape, lambda *_: (0,) * nd)


NN = (((1,), (0,)), ((), ()))
NT = (((1,), (1,)), ((), ()))
TN = (((0,), (0,)), ((), ()))


def _dg(a, b, dims):
    return lax.dot_general(a, b, dims, preferred_element_type=F32)


def _split2(x):
    hi = x.astype(BF16)
    lo = (x - hi.astype(F32)).astype(BF16)
    return hi, lo


def _split3(x):
    hi = x.astype(BF16)
    r = x - hi.astype(F32)
    mid = r.astype(BF16)
    lo = (r - mid.astype(F32)).astype(BF16)
    return hi, mid, lo


def _mm(a, b, dims=NN, passes=1):
    if passes == 1:
        return _dg(a.astype(BF16), b.astype(BF16), dims)
    ah, al = _split2(a)
    bh, bl = _split2(b)
    return _dg(ah, bh, dims) + (_dg(al, bh, dims) + _dg(ah, bl, dims))


def _mm_lconst(c, x, dims=NN):
    cb = c.astype(BF16)
    h, m, l = _split3(x)
    return _dg(cb, h, dims) + (_dg(cb, m, dims) + _dg(cb, l, dims))


def _mm_rconst(x, c, dims=NN):
    cb = c.astype(BF16)
    h, m, l = _split3(x)
    return _dg(h, cb, dims) + (_dg(m, cb, dims) + _dg(l, cb, dims))


def _mm_hl(ch, cl, x, dims=NN):
    xh, xl = _split2(x)
    return _dg(ch, xh, dims) + (_dg(ch, xl, dims) + _dg(cl, xh, dims))


def _rms(x, g):
    return x * lax.rsqrt(jnp.mean(x * x, axis=-1, keepdims=True) + NORM_EPS) * g


def _sigmoid(x):
    return 1.0 / (1.0 + jnp.exp(-x))


def _silu(x):
    return x * _sigmoid(x)


def _softplus(x):
    return jnp.maximum(x, 0.0) + jnp.log(1.0 + jnp.exp(-jnp.abs(x)))


def _shift_rows(p, prev_row, next_row):
    n = p.shape[0]
    rows = lax.broadcasted_iota(jnp.int32, p.shape, 0)
    prev = jnp.where(rows == 0, prev_row, pltpu.roll(p, 1, 0))
    nxt = jnp.where(rows == n - 1, next_row, pltpu.roll(p, n - 1, 0))
    return prev, nxt


def _halo_specs(tm, width, L, col_block=0):
    r = tm // SUBLANES
    last = L // SUBLANES - 1

    def cur(b, i):
        return (b, i, col_block)

    def prev(b, i):
        return (b, jnp.maximum(i * r - 1, 0), col_block)

    def nxt(b, i):
        return (b, jnp.minimum((i + 1) * r, last), col_block)

    return (pl.BlockSpec((1, tm, width), cur),
            pl.BlockSpec((1, SUBLANES, width), prev),
            pl.BlockSpec((1, SUBLANES, width), nxt))


def _halo_rows(prev_ref, next_ref):
    i = pl.program_id(1)
    n = pl.num_programs(1)
    prev_row = jnp.where(i > 0, prev_ref[0, SUBLANES - 1:SUBLANES, :], 0.0)
    next_row = jnp.where(i < n - 1, next_ref[0, 0:1, :], 0.0)
    return prev_row, next_row


def _head_sum_matrix(width, head_dim, scale=1.0):
    idx = np.arange(width) // head_dim
    return jnp.asarray((idx[:, None] == idx[None, :]).astype(np.float32) * scale)


def _ffn_body(x_ref, g_ref, w1_ref, w3_ref, w2_ref, gf_ref, o_ref, acc_ref, *, n_chunks, tf, final):
    x = x_ref[...]
    h = _rms(x, g_ref[...]).astype(BF16)
    for j in range(n_chunks):
        sl = slice(j * tf, (j + 1) * tf)
        a = _dg(h, w1_ref[:, sl], NN)
        b = _dg(h, w3_ref[:, sl], NN)
        t = (_silu(a) * b).astype(BF16)
        part = _dg(t, w2_ref[sl, :], NN)
        if j == 0:
            acc_ref[...] = part
        else:
            acc_ref[...] += part
    y = x + 0.5 * acc_ref[...]
    if final:
        y = _rms(y, gf_ref[...])
    o_ref[...] = y


def _ffn(x, g, w1, w3, w2, gf, final):
    T, D = x.shape
    FF = w1.shape[1]
    tm = min(512, T)
    tf = MXU_WIDTH
    body = functools.partial(_ffn_body, n_chunks=FF // tf, tf=tf, final=final)
    return pl.pallas_call(
        body,
        out_shape=jax.ShapeDtypeStruct((T, D), F32),
        grid=(T // tm,),
        in_specs=[pl.BlockSpec((tm, D), lambda i: (i, 0)), _full((1, D)),
                  _full((D, FF)), _full((D, FF)), _full((FF, D)), _full((1, D))],
        out_specs=pl.BlockSpec((tm, D), lambda i: (i, 0)),
        scratch_shapes=[pltpu.VMEM((tm, D), F32)],
        compiler_params=_cparams("parallel"),
        name="ffn_final" if final else "ffn",
    )(x, g.reshape(1, D), w1, w3, w2, gf.reshape(1, D))


def _inproj_body(x_ref, g_ref, w_ref, ohy_ref, orw_ref, ogd_ref):
    h = _rms(x_ref[...], g_ref[...]).astype(BF16)
    c0, c1 = HY_COLS, HY_COLS + RW_COLS
    ohy_ref[...] = _dg(h, w_ref[:, 0:c0], NN)
    orw_ref[...] = _dg(h, w_ref[:, c0:c1], NN)
    ogd_ref[...] = _dg(h, w_ref[:, c1:c1 + GDN_COLS_PAD], NN)


def _inproj(x, g, w):
    T, D = x.shape
    tm = min(256, T)
    W = w.shape[1]
    widths = (HY_COLS, RW_COLS, GDN_COLS_PAD)
    return pl.pallas_call(
        _inproj_body,
        out_shape=tuple(jax.ShapeDtypeStruct((T, n), F32) for n in widths),
        grid=(T // tm,),
        in_specs=[pl.BlockSpec((tm, D), lambda i: (i, 0)), _full((1, D)), _full((D, W))],
        out_specs=tuple(pl.BlockSpec((tm, n), lambda i: (i, 0)) for n in widths),
        compiler_params=_cparams("parallel"),
        name="inproj",
    )(x, g.reshape(1, D), w)


def _hy_filter_body(z_ref, t_ref, freq_ref, w1_ref, b1_ref, w2_ref, b2_ref, w3_ref, dec_ref,
                    h_ref, s_ref):
    i = pl.program_id(0)
    freq = freq_ref[...]
    h = jnp.sin(freq * (_mm(z_ref[...], w1_ref[...], passes=3) + b1_ref[...]))
    h = jnp.sin(freq * (_mm(h, w2_ref[...], passes=3) + b2_ref[...]))
    h = _mm(h, w3_ref[...], passes=3) * jnp.exp(-t_ref[...] * dec_ref[...])
    C = D_HYENA
    fwd = h[:, :C]
    bwd = h[:, C:]
    rows = lax.broadcasted_iota(jnp.int32, bwd.shape, 0)
    bwd = jnp.where((rows == 0) & (i == 0), 0.0, bwd)
    h_ref[0] = fwd
    h_ref[1] = bwd
    part = jnp.sum(jnp.abs(fwd) + jnp.abs(bwd), axis=0, keepdims=True)

    @pl.when(i == 0)
    def _():
        s_ref[...] = part

    @pl.when(i > 0)
    def _():
        s_ref[...] += part


def _hy_filter(z, t, freq, w1, b1, w2, b2, w3, decay):
    L = z.shape[0]
    tl = min(1024, L)
    C = D_HYENA
    return pl.pallas_call(
        _hy_filter_body,
        out_shape=(jax.ShapeDtypeStruct((2, L, C), F32), jax.ShapeDtypeStruct((1, C), F32)),
        grid=(L // tl,),
        in_specs=[pl.BlockSpec((tl, LANES), lambda i: (i, 0)), pl.BlockSpec((tl, 1), lambda i: (i, 0)),
                  _full((1, HY_FFN)), _full((LANES, HY_FFN)), _full((1, HY_FFN)),
                  _full((HY_FFN, HY_FFN)), _full((1, HY_FFN)), _full((HY_FFN, 2 * C)), _full((1, 2 * C))],
        out_specs=(pl.BlockSpec((2, tl, C), lambda i: (0, i, 0)), _full((1, C))),
        compiler_params=_cparams("arbitrary"),
        name="hy_filter",
    )(z, t, freq.reshape(1, -1), w1, b1.reshape(1, -1), w2, b2.reshape(1, -1), w3, decay.reshape(1, -1))


def _hy_pre_body(p_ref, pp_ref, pn_ref, w_ref, b_ref, u_ref, x0_ref):
    p = p_ref[0]
    prev_row, next_row = _halo_rows(pp_ref, pn_ref)
    prev, nxt = _shift_rows(p, prev_row, next_row)
    c = w_ref[0:1, :] * prev + w_ref[1:2, :] * p + w_ref[2:3, :] * nxt + b_ref[...]
    C = D_HYENA
    x0_ref[0] = c[:, :C]
    u_ref[0] = c[:, C:2 * C] * c[:, 2 * C:]


def _hy_pre(p, w, b):
    B, L, W = p.shape
    tm = min(512, L)
    C = D_HYENA
    return pl.pallas_call(
        _hy_pre_body,
        out_shape=(jax.ShapeDtypeStruct((B, L, C), F32),) * 2,
        grid=(B, L // tm),
        in_specs=[*_halo_specs(tm, W, L), _full((3, W)), _full((1, W))],
        out_specs=(pl.BlockSpec((1, tm, C), lambda b, i: (b, i, 0)),) * 2,
        compiler_params=_cparams("parallel", "parallel"),
        name="hy_pre",
    )(p, p, p, w, b.reshape(1, W))


def _fft_a_body(u_ref, mh_ref, ml_ref, twr_ref, twi_ref, o_ref, *, na):
    a = _mm_hl(mh_ref[...], ml_ref[...], u_ref[0])
    ar, ai = a[:na], a[na:]
    twr, twi = twr_ref[...], twi_ref[...]
    o_ref[0, 0] = ar * twr - ai * twi
    o_ref[0, 1] = ar * twi + ai * twr


def _fft_a(u2d, mh, ml, twr, twi):
    B, half, cols = u2d.shape
    na = 2 * half
    tn = min(2048, cols)
    return pl.pallas_call(
        functools.partial(_fft_a_body, na=na),
        out_shape=jax.ShapeDtypeStruct((B, 2, na, cols), F32),
        grid=(B, cols // tn),
        in_specs=[pl.BlockSpec((1, half, tn), lambda b, j: (b, 0, j)),
                  _full((2 * na, half)), _full((2 * na, half)),
                  pl.BlockSpec((na, tn), lambda b, j: (0, j)), pl.BlockSpec((na, tn), lambda b, j: (0, j))],
        out_specs=pl.BlockSpec((1, 2, na, tn), lambda b, j: (b, 0, 0, j)),
        compiler_params=_cparams("parallel", "parallel"),
        name="fft_a",
    )(u2d, mh, ml, twr, twi)


def _fft_filt_body(bf_ref, bb_ref, mh_ref, ml_ref, s_ref, o_ref, *, tk):
    nb = FFT_NB
    inv = 1.0 / s_ref[...]
    for k in range(tk):
        xf = _mm_hl(mh_ref[...], ml_ref[...], jnp.concatenate([bf_ref[0, 0, k], bf_ref[0, 1, k]], axis=0))
        xb = _mm_hl(mh_ref[...], ml_ref[...], jnp.concatenate([bb_ref[0, 0, k], bb_ref[0, 1, k]], axis=0))
        o_ref[0, k] = (xf[:nb] + xb[:nb]) * inv
        o_ref[1, k] = (xf[nb:] - xb[nb:]) * inv


def _fft_filt(bt, mh, ml, s):
    _, _, na, nb, C = bt.shape
    tk = min(4, na)
    return pl.pallas_call(
        functools.partial(_fft_filt_body, tk=tk),
        out_shape=jax.ShapeDtypeStruct((2, na, nb, C), F32),
        grid=(na // tk,),
        in_specs=[pl.BlockSpec((1, 2, tk, nb, C), lambda i: (0, 0, i, 0, 0)),
                  pl.BlockSpec((1, 2, tk, nb, C), lambda i: (1, 0, i, 0, 0)),
                  _full((2 * nb, 2 * nb)), _full((2 * nb, 2 * nb)), _full((1, C))],
        out_specs=pl.BlockSpec((2, tk, nb, C), lambda i: (0, i, 0, 0)),
        compiler_params=_cparams("parallel"),
        name="fft_filt",
    )(bt, bt, mh, ml, s)


def _fft_c_body(b_ref, k_ref, mfh_ref, mfl_ref, mih_ref, mil_ref, twr_ref, twi_ref, o_ref, *, tk):
    nb = FFT_NB
    for k in range(tk):
        x = _mm_hl(mfh_ref[...], mfl_ref[...], jnp.concatenate([b_ref[0, 0, k], b_ref[0, 1, k]], axis=0))
        xr, xi = x[:nb], x[nb:]
        kr, ki = k_ref[0, k], k_ref[1, k]
        y = jnp.concatenate([xr * kr - xi * ki, xr * ki + xi * kr], axis=0)
        d = _mm_hl(mih_ref[...], mil_ref[...], y)
        dr, di = d[:nb], d[nb:]
        reps = dr.shape[1] // LANES
        twr = jnp.concatenate([twr_ref[k]] * reps, axis=1)
        twi = jnp.concatenate([twi_ref[k]] * reps, axis=1)
        o_ref[0, 0, k] = dr * twr + di * twi
        o_ref[0, 1, k] = di * twr - dr * twi


def _fft_c(bt, khat, mfh, mfl, mih, mil, twr, twi):
    B, _, na, nb, C = bt.shape
    tk = min(4, na)
    blk = pl.BlockSpec((1, 2, tk, nb, C), lambda b, i: (b, 0, i, 0, 0))
    m = _full((2 * nb, 2 * nb))
    tw = pl.BlockSpec((tk, nb, LANES), lambda b, i: (i, 0, 0))
    return pl.pallas_call(
        functools.partial(_fft_c_body, tk=tk),
        out_shape=jax.ShapeDtypeStruct(bt.shape, F32),
        grid=(B, na // tk),
        in_specs=[blk, pl.BlockSpec((2, tk, nb, C), lambda b, i: (0, i, 0, 0)), m, m, m, m, tw, tw],
        out_specs=blk,
        compiler_params=_cparams("parallel", "parallel"),
        name="fft_c",
    )(bt, khat, mfh, mfl, mih, mil, twr, twi)


def _fft_out_body(e_ref, mh_ref, ml_ref, u_ref, x0_ref, bias_ref, o_ref):
    conv = _mm_hl(mh_ref[...], ml_ref[...], e_ref[0])
    u = u_ref[0]
    o_ref[0] = x0_ref[0] * (conv + bias_ref[...] * u)


def _fft_out(e2d, mh, ml, u2d, x02d, bias_t):
    B, half, cols = u2d.shape
    tn = min(2048, cols)
    blk = pl.BlockSpec((1, half, tn), lambda b, j: (b, 0, j))
    return pl.pallas_call(
        _fft_out_body,
        out_shape=jax.ShapeDtypeStruct(u2d.shape, F32),
        grid=(B, cols // tn),
        in_specs=[pl.BlockSpec((1, 4 * half, tn), lambda b, j: (b, 0, j)),
                  _full((half, 4 * half)), _full((half, 4 * half)), blk, blk,
                  pl.BlockSpec((1, tn), lambda b, j: (0, j))],
        out_specs=blk,
        compiler_params=_cparams("parallel", "parallel"),
        name="fft_out",
    )(e2d, mh, ml, u2d, x02d, bias_t)


def _hl(m):
    m = jnp.asarray(m, F32)
    hi = m.astype(BF16)
    return hi, (m - hi.astype(F32)).astype(BF16)


@functools.lru_cache(maxsize=None)
def _fft_consts_np(L):
    n = 2 * L
    nb = FFT_NB
    na = n // nb
    half = na // 2
    ka = np.arange(na)
    fa = np.exp(-2j * np.pi * np.outer(ka, ka) / na)
    kb = np.arange(nb)
    fb = np.exp(-2j * np.pi * np.outer(kb, kb) / nb)
    m_a = np.concatenate([fa.real[:, :half], fa.imag[:, :half]], axis=0)
    m_f = np.block([[fb.real, -fb.imag], [fb.imag, fb.real]])
    m_i = np.block([[fb.real, fb.imag], [-fb.imag, fb.real]])
    m_o = np.concatenate([fa.real[:half, :], fa.imag[:half, :]], axis=1) / n
    tw = np.exp(-2j * np.pi * (np.outer(ka, kb) % n) / n)
    f32 = lambda a: np.asarray(a, np.float32)
    return f32(m_a), f32(m_f), f32(m_i), f32(m_o), f32(tw.real), f32(tw.imag)


def _hyena_pos(L):
    t = jnp.linspace(0.0, 1.0, L, dtype=F32)[:, None]
    ang = 2.0 * math.pi * jnp.arange(L, dtype=F32)[:, None] / L
    bands = jnp.linspace(1e-4, HY_BANDS - 1, HY_BANDS, dtype=F32)[None, :]
    z = jnp.concatenate([t, jnp.cos(bands * ang), -jnp.sin(bands * ang)], axis=-1)
    return jnp.pad(z, ((0, 0), (0, LANES - HY_EMB))), t


def _hyena(p_hy, z, t, prm, l):
    B, L, _ = p_hy.shape
    C = D_HYENA
    nb = FFT_NB
    na = 2 * L // nb
    half = na // 2
    m_a, m_f, m_i, m_o, twr, twi = _fft_consts_np(L)
    mah, mal = _hl(m_a)
    mfh, mfl = _hl(m_f)
    mih, mil = _hl(m_i)
    moh, mol = _hl(m_o)
    twr_cols = jnp.repeat(jnp.asarray(twr), C, axis=1)
    twi_cols = jnp.repeat(jnp.asarray(twi), C, axis=1)
    twr_l = jnp.broadcast_to(jnp.asarray(twr)[:, :, None], (na, nb, LANES))
    twi_l = jnp.broadcast_to(jnp.asarray(twi)[:, :, None], (na, nb, LANES))

    w1 = jnp.pad(prm["hy_w1"][l], ((0, LANES - HY_EMB), (0, 0)))
    hraw, hsum = _hy_filter(z, t, prm["hy_freq"][l], w1, prm["hy_b1"][l], prm["hy_w2"][l],
                            prm["hy_b2"][l], prm["hy_w3"][l], prm["hy_decay"][l])
    hb = _fft_a(hraw.reshape(2, half, nb * C), mah, mal, twr_cols, twi_cols)
    khat = _fft_filt(hb.reshape(2, 2, na, nb, C), mfh, mfl, hsum)

    u, x0 = _hy_pre(p_hy, prm["hy_conv_w"][l], prm["hy_conv_b"][l])
    u2d = u.reshape(B, half, nb * C)
    bt = _fft_a(u2d, mah, mal, twr_cols, twi_cols)
    e = _fft_c(bt.reshape(B, 2, na, nb, C), khat, mfh, mfl, mih, mil, twr_l, twi_l)
    bias_t = jnp.tile(prm["hy_bias"][l], nb).reshape(1, nb * C)
    y = _fft_out(e.reshape(B, 2 * na, nb * C), moh, mol, u2d, x0.reshape(B, half, nb * C), bias_t)
    return y.reshape(B * L, C)


def _pair_masks():
    lane = lax.broadcasted_iota(jnp.int32, (1, PAIR), 1)
    return (lane < HEAD_DIM).astype(F32), (lane >= HEAD_DIM).astype(F32)


def _sm(x, m0, m1):
    return jnp.concatenate([x * m0, x * m1], axis=0)


def _tri_masks(reverse):
    n = 2 * CHUNK
    r = lax.broadcasted_iota(jnp.int32, (n, n), 0)
    c = lax.broadcasted_iota(jnp.int32, (n, n), 1)
    same = (r // CHUNK) == (c // CHUNK)
    if reverse:
        return same & (c > r), same & (c >= r)
    return same & (c < r), same & (c <= r)


def _cum_matrix(reverse):
    r = lax.broadcasted_iota(jnp.int32, (CHUNK, CHUNK), 0)
    c = lax.broadcasted_iota(jnp.int32, (CHUNK, CHUNK), 1)
    return ((c >= r) if reverse else (c <= r)).astype(F32)


NEUMANN_PASSES = 1
GRAM_PASSES = 1


def _neumann_inverse(a):
    n = a.shape[0]
    r = lax.broadcasted_iota(jnp.int32, (n, n), 0)
    c = lax.broadcasted_iota(jnp.int32, (n, n), 1)
    t = jnp.where(r == c, 1.0, 0.0) + a
    p = a
    steps = int(math.log2(CHUNK)) - 1
    for _ in range(steps):
        p = _mm(p, p, passes=NEUMANN_PASSES)
        yield
        t = t + _mm(t, p, passes=NEUMANN_PASSES)
    return t


def _interleave(gens):
    results = [None] * len(gens)
    alive = list(range(len(gens)))
    while alive:
        for i in list(alive):
            try:
                next(gens[i])
            except StopIteration as done:
                results[i] = done.value
                alive.remove(i)
    return results


def _rw_prep_body(p_ref, pp_ref, pn_ref, mup_ref, mun_ref, wl_ref, w0_ref, al_ref, a0_ref, gl_ref,
                  kk_ref, ka_ref, rk_ref, hs_ref,
                  r_o, v_o, kkn_o, gate_o, bonus_o, lwf_o, kdf_o, af_o, lwb_o, kdb_o, ab_o):
    p = p_ref[0]
    prev_row, next_row = _halo_rows(pp_ref, pn_ref)
    prev, nxt = _shift_rows(p, prev_row, next_row)
    p = p + mup_ref[...] * (prev - p) + mun_ref[...] * (nxt - p)
    D = D_RWKV
    r, k, v = p[:, 0:D], p[:, D:2 * D], p[:, 2 * D:3 * D]
    o = 3 * D
    lw = (p[:, o:o + RW_LORA_W], p[:, o + RW_LORA_W:o + 2 * RW_LORA_W])
    o += 2 * RW_LORA_W
    la = (p[:, o:o + RW_LORA_A], p[:, o + RW_LORA_A:o + 2 * RW_LORA_A])
    o += 2 * RW_LORA_A
    lg = p[:, o:o + RW_LORA_G]
    hs = hs_ref[...]
    gate_o[0] = _mm(_sigmoid(lg), gl_ref[...], passes=3)
    k2 = k * kk_ref[...]
    kkn = k2 * lax.rsqrt(_mm_rconst(k2 * k2, hs) + L2_EPS)
    r_o[0] = r
    v_o[0] = v
    kkn_o[0] = kkn
    bonus_o[0] = _mm_rconst(r * k * rk_ref[...], hs) * v
    outs = ((lwf_o, kdf_o, af_o), (lwb_o, kdb_o, ab_o))
    for d in range(2):
        lw_o, kd_o, a_o = outs[d]
        logw = -RW_DECAY_SCALE * _sigmoid(w0_ref[d:d + 1, :] + _mm(jnp.tanh(lw[d]), wl_ref[d], passes=3))
        a = _sigmoid(a0_ref[d:d + 1, :] + _mm(la[d], al_ref[d], passes=3))
        lw_o[0] = logw
        kd_o[0] = k * (1.0 + (a - 1.0) * ka_ref[...])
        a_o[0] = a


def _rw_prep(p, prm, l):
    B, L, W = p.shape
    tm = min(256, L)
    D = D_RWKV
    row = lambda a: a.reshape(1, -1)
    out = pl.BlockSpec((1, tm, D), lambda b, i: (b, i, 0))
    return pl.pallas_call(
        _rw_prep_body,
        out_shape=(jax.ShapeDtypeStruct((B, L, D), F32),) * 11,
        grid=(B, L // tm),
        in_specs=[*_halo_specs(tm, W, L), _full((1, W)), _full((1, W)),
                  _full((2, RW_LORA_W, D)), _full((2, D)), _full((2, RW_LORA_A, D)), _full((2, D)),
                  _full((RW_LORA_G, D)), _full((1, D)), _full((1, D)), _full((1, D)), _full((D, D))],
        out_specs=(out,) * 11,
        compiler_params=_cparams("parallel", "parallel"),
        name="rw_prep",
    )(p, p, p, row(prm["rw_mu_prev"][l]), row(prm["rw_mu_next"][l]), prm["rw_w_lora"][l], prm["rw_w0"][l],
      prm["rw_a_lora"][l], prm["rw_a0"][l], prm["rw_g_lora"][l], row(prm["rw_k_k"][l]),
      row(prm["rw_k_a"][l]), row(prm["rw_r_k"][l]), _head_sum_matrix(D, RW_HEAD_DIM))


def _rw_chunk(r, v, kk, lw, kd, a, s0, reverse, m0, m1):
    C = CHUNK
    cum = _mm_lconst(_cum_matrix(reverse), lw)
    yield
    pin = jnp.exp(cum)
    at = _sm(-kk * jnp.exp(cum - lw), m0, m1)
    rt = _sm(r * pin, m0, m1)
    pinv = jnp.exp(-cum)
    kh = _sm(kd * pinv, m0, m1)
    bh = _sm(kk * a * pinv, m0, m1)
    vs = _sm(v, m0, m1)
    tot = cum[0:1, :] if reverse else cum[C - 1:C, :]
    strict, incl = _tri_masks(reverse)
    g = _mm(jnp.concatenate([at, rt], axis=0), jnp.concatenate([kh, bh], axis=0), NT, passes=GRAM_PASSES)
    yield
    n = 2 * C
    a_ak = jnp.where(strict, g[:n, :n], 0.0)
    a_ab = jnp.where(strict, g[:n, n:], 0.0)
    a_rk = jnp.where(incl, g[n:, :n], 0.0)
    a_rb = jnp.where(incl, g[n:, n:], 0.0)
    av = _mm(jnp.concatenate([a_ak, a_rk], axis=0), vs)
    t = yield from _neumann_inverse(a_ab)
    yield
    uw = _mm(t, jnp.concatenate([av[:n], at], axis=1))
    yield
    ws = _mm(jnp.concatenate([uw[:, PAIR:], rt], axis=0), s0, NT)
    yield
    u = uw[:, :PAIR] + ws[:n]
    y = ws[n:] + av[n:] + _mm(a_rb, u)
    upd = _mm(jnp.concatenate([vs, u], axis=0), jnp.concatenate([kh, bh], axis=0), TN)
    s_new = (s0 + upd) * jnp.exp(tot)
    return y[:C] + y[C:], s_new


def _rw_scan_body(rf, vf, kkf, lwf, kdf, af, rb, vb, kkb, lwb, kdb, ab, yf_o, yb_o, s_ref):
    @pl.when(pl.program_id(1) == 0)
    def _():
        s_ref[...] = jnp.zeros(s_ref.shape, F32)

    m0, m1 = _pair_masks()
    dirs = ((rf, vf, kkf, lwf, kdf, af, yf_o, False), (rb, vb, kkb, lwb, kdb, ab, yb_o, True))
    gens, outs = [], []
    for d, (r, v, kk, lw, kd, a, y_o, rev) in enumerate(dirs):
        for j in range(D_RWKV // PAIR):
            sl = slice(j * PAIR, (j + 1) * PAIR)
            gens.append(_rw_chunk(r[0, :, sl], v[0, :, sl], kk[0, :, sl], lw[0, :, sl], kd[0, :, sl],
                                  a[0, :, sl], s_ref[d, j], rev, m0, m1))
            outs.append((y_o, sl, d, j))
    for (y, s_new), (y_o, sl, d, j) in zip(_interleave(gens), outs):
        y_o[0, :, sl] = y
        s_ref[d, j] = s_new


def _rw_scan(r, v, kk, lwf, kdf, af, lwb, kdb, ab):
    B, L, D = r.shape
    n = L // CHUNK
    fw = pl.BlockSpec((1, CHUNK, D), lambda b, i: (b, i, 0))
    bw = pl.BlockSpec((1, CHUNK, D), lambda b, i: (b, n - 1 - i, 0))
    return pl.pallas_call(
        _rw_scan_body,
        out_shape=(jax.ShapeDtypeStruct((B, L, D), F32),) * 2,
        grid=(B, n),
        in_specs=[fw] * 6 + [bw] * 6,
        out_specs=(fw, bw),
        scratch_shapes=[pltpu.VMEM((2, D // PAIR, PAIR, PAIR), F32)],
        compiler_params=_cparams("parallel", "arbitrary"),
        name="rw_scan",
    )(r, v, kk, lwf, kdf, af, r, v, kk, lwb, kdb, ab)


GDN_G_LANE = 0
GDN_BETA_LANE = 2 * GDN_HEADS


def _gdn_prep_body(p_ref, pp_ref, pn_ref, s_ref, w_ref, alog_ref, dt_ref, hs_ref, q_o, k_o, v_o, gs_o):
    p = p_ref[0]
    prev_row, next_row = _halo_rows(pp_ref, pn_ref)
    prev, nxt = _shift_rows(p, prev_row, next_row)
    c = _silu(w_ref[0:1, :] * prev + w_ref[1:2, :] * p + w_ref[2:3, :] * nxt)
    D = D_GDN
    q, k, v = c[:, 0:D], c[:, D:2 * D], c[:, 2 * D:3 * D]
    hs = hs_ref[...]
    q_o[0] = q * lax.rsqrt(_mm_rconst(q * q, hs) + L2_EPS) * (GDN_HEAD_DIM ** -0.5)
    k_o[0] = k * lax.rsqrt(_mm_rconst(k * k, hs) + L2_EPS)
    v_o[0] = v
    s = s_ref[0]
    lane = lax.broadcasted_iota(jnp.int32, s.shape, 1)
    g = -jnp.exp(alog_ref[...]) * _softplus(s + dt_ref[...])
    gs_o[0] = jnp.where(lane < GDN_BETA_LANE, g, jnp.where(lane < 4 * GDN_HEADS, _sigmoid(s), 0.0))


def _gdn_prep(p, prm, l):
    B, L, W = p.shape
    tm = min(256, L)
    D = D_GDN
    Wq = 3 * D
    pad = LANES - 2 * GDN_HEADS
    alog = jnp.pad(prm["gdn_a_log"][l].reshape(-1), (0, pad)).reshape(1, LANES)
    dt = jnp.pad(prm["gdn_dt_bias"][l].reshape(-1), (0, pad)).reshape(1, LANES)
    out = pl.BlockSpec((1, tm, D), lambda b, i: (b, i, 0))
    side = pl.BlockSpec((1, tm, LANES), lambda b, i: (b, i, 4 * D // LANES))
    return pl.pallas_call(
        _gdn_prep_body,
        out_shape=(jax.ShapeDtypeStruct((B, L, D), F32),) * 3 + (jax.ShapeDtypeStruct((B, L, LANES), F32),),
        grid=(B, L // tm),
        in_specs=[*_halo_specs(tm, Wq, L), side, _full((3, Wq)), _full((1, LANES)), _full((1, LANES)),
                  _full((D, D))],
        out_specs=(out,) * 3 + (pl.BlockSpec((1, tm, LANES), lambda b, i: (b, i, 0)),),
        compiler_params=_cparams("parallel", "parallel"),
        name="gdn_prep",
    )(p, p, p, p, prm["gdn_conv_w"][l], alog, dt, _head_sum_matrix(D, GDN_HEAD_DIM))


def _col_pair(x, lane0):
    return jnp.concatenate([x[:, lane0:lane0 + 1], x[:, lane0 + 1:lane0 + 2]], axis=0)


def _gdn_chunk(q, k, v, gs, gcum, d, j, s0, reverse, m0, m1):
    C = CHUNK
    n = 2 * C
    h0 = 2 * j
    g_lane = GDN_G_LANE + d * GDN_HEADS + h0
    b_lane = GDN_BETA_LANE + d * GDN_HEADS + h0
    gc = _col_pair(gcum, g_lane)
    beta = _col_pair(gs, b_lane)
    lane = lax.broadcasted_iota(jnp.int32, (1, LANES), 1)
    sel = jnp.concatenate([jnp.broadcast_to((lane == g_lane).astype(F32), (C, LANES)),
                           jnp.broadcast_to((lane == g_lane + 1).astype(F32), (C, LANES))], axis=0)
    gq = jnp.concatenate([gcum, gcum], axis=0) * sel
    ones = jnp.ones((n, LANES), F32)
    gcol = _mm_rconst(gq, ones, NT)
    grow = _mm_lconst(ones, gq, NT)
    ks = _sm(k, m0, m1)
    qs = _sm(q, m0, m1)
    vs = _sm(v, m0, m1)
    kb = ks * beta
    kq = _mm(jnp.concatenate([kb, qs], axis=0), ks, NT, passes=GRAM_PASSES)
    yield
    strict, incl = _tri_masks(reverse)
    dec = jnp.exp(jnp.where(incl, gcol - grow, -1e30))
    a = jnp.where(strict, kq[:n] * dec, 0.0)
    attn = kq[n:] * dec
    t = yield from _neumann_inverse(-a)
    yield
    egc = jnp.exp(gc)
    sol = _mm(t, jnp.concatenate([vs * beta, kb * egc], axis=1))
    yield
    ws = _mm(jnp.concatenate([sol[:, PAIR:], qs * egc], axis=0), s0)
    yield
    v_new = sol[:, :PAIR] - ws[:n]
    o = ws[n:] + _mm(attn, v_new)
    glast = gc[0:1] if reverse else gc[C - 1:C]
    glast1 = gc[C:C + 1] if reverse else gc[n - 1:n]
    rows = lax.broadcasted_iota(jnp.int32, (n, 1), 0)
    gl = jnp.where(rows < C, glast, glast1)
    upd = _mm(ks * jnp.exp(gl - gc), v_new, TN)
    rows_s = lax.broadcasted_iota(jnp.int32, (PAIR, 1), 0)
    s_new = s0 * jnp.exp(jnp.where(rows_s < HEAD_DIM, glast, glast1)) + upd
    return o[:C] + o[C:], s_new


def _gdn_scan_body(qf, kf, vf, gf, qb, kb, vb, gb, of_o, ob_o, s_ref):
    @pl.when(pl.program_id(1) == 0)
    def _():
        s_ref[...] = jnp.zeros(s_ref.shape, F32)

    m0, m1 = _pair_masks()
    dirs = ((qf, kf, vf, gf, of_o, False), (qb, kb, vb, gb, ob_o, True))
    gens, outs = [], []
    for d, (q, k, v, g, o_o, rev) in enumerate(dirs):
        gs = g[0]
        gcum = _mm_lconst(_cum_matrix(rev), gs)
        for j in range(D_GDN // PAIR):
            sl = slice(j * PAIR, (j + 1) * PAIR)
            gens.append(_gdn_chunk(q[0, :, sl], k[0, :, sl], v[0, :, sl], gs, gcum, d, j, s_ref[d, j],
                                   rev, m0, m1))
            outs.append((o_o, sl, d, j))
    for (o, s_new), (o_o, sl, d, j) in zip(_interleave(gens), outs):
        o_o[0, :, sl] = o
        s_ref[d, j] = s_new


def _gdn_scan(q, k, v, gs):
    B, L, D = q.shape
    n = L // CHUNK
    fw = pl.BlockSpec((1, CHUNK, D), lambda b, i: (b, i, 0))
    bw = pl.BlockSpec((1, CHUNK, D), lambda b, i: (b, n - 1 - i, 0))
    fws = pl.BlockSpec((1, CHUNK, LANES), lambda b, i: (b, i, 0))
    bws = pl.BlockSpec((1, CHUNK, LANES), lambda b, i: (b, n - 1 - i, 0))
    return pl.pallas_call(
        _gdn_scan_body,
        out_shape=(jax.ShapeDtypeStruct((B, L, D), F32),) * 2,
        grid=(B, n),
        in_specs=[fw, fw, fw, fws, bw, bw, bw, bws],
        out_specs=(fw, bw),
        scratch_shapes=[pltpu.VMEM((2, D // PAIR, PAIR, PAIR), F32)],
        compiler_params=_cparams("parallel", "arbitrary"),
        name="gdn_scan",
    )(q, k, v, gs, q, k, v, gs)


def _mix_out_body(x_ref, yhy_ref, yf_ref, yb_ref, gate_ref, bonus_ref, of_ref, ob_ref, zg_ref,
                  gnw_ref, gnb_ref, nw_ref, avg_ref, w_ref, o_ref):
    avg = avg_ref[...]
    y = yf_ref[...] + yb_ref[...]
    mu = _mm_rconst(y, avg)
    dlt = y - mu
    var = _mm_rconst(dlt * dlt, avg)
    y_rw = (dlt * lax.rsqrt(var + RW_GN_EPS) * gnw_ref[...] + gnb_ref[...] + bonus_ref[...]) * gate_ref[...]
    o = of_ref[...] + ob_ref[...]
    ms = _mm_rconst(o * o, avg)
    y_gdn = o * lax.rsqrt(ms + NORM_EPS) * nw_ref[...] * _silu(zg_ref[...])
    c0, c1 = D_HYENA, D_HYENA + D_RWKV
    acc = _dg(yhy_ref[...].astype(BF16), w_ref[0:c0, :], NN)
    acc += _dg(y_rw.astype(BF16), w_ref[c0:c1, :], NN)
    acc += _dg(y_gdn.astype(BF16), w_ref[c1:, :], NN)
    o_ref[...] = x_ref[...] + acc


def _mix_out(x, y_hy, yf, yb, gate, bonus, of, ob, p_gdn, prm, l, w_out):
    T, D = x.shape
    tm = min(512, T)
    Dh = D_RWKV
    row = lambda a: a.reshape(1, -1)
    tile = lambda n: pl.BlockSpec((tm, n), lambda i: (i, 0))
    zg = pl.BlockSpec((tm, D_GDN), lambda i: (i, 3))
    avg = _head_sum_matrix(Dh, HEAD_DIM, 1.0 / HEAD_DIM)
    return pl.pallas_call(
        _mix_out_body,
        out_shape=jax.ShapeDtypeStruct((T, D), F32),
        grid=(T // tm,),
        in_specs=[tile(D), tile(D_HYENA)] + [tile(Dh)] * 6 + [zg, _full((1, Dh)), _full((1, Dh)),
                  _full((1, Dh)), _full((Dh, Dh)), _full((D, D))],
        out_specs=tile(D),
        compiler_params=_cparams("parallel"),
        name="mix_out",
    )(x, y_hy, yf, yb, gate, bonus, of, ob, p_gdn, row(prm["rw_gn_w"][l]), row(prm["rw_gn_b"][l]),
      row(jnp.tile(prm["gdn_norm_w"][l], GDN_HEADS)), avg, w_out)


def _mem_kv_body(m_ref, g_ref, wk_ref, wv_ref, k_o, v_o):
    h = _rms(m_ref[0], g_ref[...]).astype(BF16)
    k_o[0] = _dg(h, wk_ref[...], NN).astype(BF16)
    v_o[0] = _dg(h, wv_ref[...], NN).astype(BF16)


def _mem_kv(mem, g, wk, wv):
    B, M, D = mem.shape
    blk = pl.BlockSpec((1, M, D), lambda b: (b, 0, 0))
    return pl.pallas_call(
        _mem_kv_body,
        out_shape=(jax.ShapeDtypeStruct((B, M, D), BF16),) * 2,
        grid=(B,),
        in_specs=[blk, _full((1, D)), _full((D, D)), _full((D, D))],
        out_specs=(blk, blk),
        compiler_params=_cparams("parallel"),
        name="mem_kv",
    )(mem, g.reshape(1, D), wk, wv)


def _xattn_body(x_ref, g_ref, wq_ref, k_ref, v_ref, wo_ref, o_ref):
    x = x_ref[0]
    h = _rms(x, g_ref[...]).astype(BF16)
    q = (_dg(h, wq_ref[...], NN) * (XA_HEAD_DIM ** -0.5)).astype(BF16)
    outs = []
    for hd in range(XA_HEADS):
        sl = slice(hd * XA_HEAD_DIM, (hd + 1) * XA_HEAD_DIM)
        s = _dg(q[:, sl], k_ref[0, :, sl], NT)
        s = s - jnp.max(s, axis=-1, keepdims=True)
        e = jnp.exp(s)
        pr = e / jnp.sum(e, axis=-1, keepdims=True)
        outs.append(_dg(pr.astype(BF16), v_ref[0, :, sl], NN))
    o = jnp.concatenate(outs, axis=1).astype(BF16)
    o_ref[0] = x + _dg(o, wo_ref[...], NN)


def _xattn(x, g, wq, k, v, wo):
    B, L, D = x.shape
    M = k.shape[1]
    tm = min(512, L)
    tile = pl.BlockSpec((1, tm, D), lambda b, i: (b, i, 0))
    kv = pl.BlockSpec((1, M, D), lambda b, i: (b, 0, 0))
    return pl.pallas_call(
        _xattn_body,
        out_shape=jax.ShapeDtypeStruct((B, L, D), F32),
        grid=(B, L // tm),
        in_specs=[tile, _full((1, D)), _full((D, D)), kv, kv, _full((D, D))],
        out_specs=tile,
        compiler_params=_cparams("parallel", "parallel"),
        name="xattn",
    )(x, g.reshape(1, D), wq, k, v, wo)


def kernel(x, mem, norm_ffn1, ffn1_w1, ffn1_w3, ffn1_w2, norm_mix, w_in, w_out, hy_conv_w, hy_conv_b, hy_freq, hy_w1, hy_b1, hy_w2, hy_b2, hy_w3, hy_decay, hy_bias, rw_mu_prev, rw_mu_next, rw_w_lora, rw_w0, rw_a_lora, rw_a0, rw_g_lora, rw_k_k, rw_k_a, rw_r_k, rw_gn_w, rw_gn_b, gdn_conv_w, gdn_a_log, gdn_dt_bias, gdn_norm_w, norm_xattn, xa_wq, xa_wk, xa_wv, xa_wo, mem_norm, norm_ffn2, ffn2_w1, ffn2_w3, ffn2_w2, norm_final):
    prm = dict(hy_conv_w=hy_conv_w, hy_conv_b=hy_conv_b, hy_freq=hy_freq, hy_w1=hy_w1, hy_b1=hy_b1,
               hy_w2=hy_w2, hy_b2=hy_b2, hy_w3=hy_w3, hy_decay=hy_decay, hy_bias=hy_bias,
               rw_mu_prev=rw_mu_prev, rw_mu_next=rw_mu_next, rw_w_lora=rw_w_lora, rw_w0=rw_w0,
               rw_a_lora=rw_a_lora, rw_a0=rw_a0, rw_g_lora=rw_g_lora, rw_k_k=rw_k_k, rw_k_a=rw_k_a,
               rw_r_k=rw_r_k, rw_gn_w=rw_gn_w, rw_gn_b=rw_gn_b, gdn_conv_w=gdn_conv_w,
               gdn_a_log=gdn_a_log, gdn_dt_bias=gdn_dt_bias, gdn_norm_w=gdn_norm_w)
    B, L, D = x.shape
    depth = norm_ffn1.shape[0]
    T = B * L
    z_pos, t_pos = _hyena_pos(L)
    bf = lambda w: w.astype(BF16)
    w_in_p = jnp.pad(w_in, ((0, 0), (0, 0), (0, GDN_COLS_PAD - GDN_COLS)))
    xt = x.reshape(T, D)
    for l in range(depth):
        xt = _ffn(xt, norm_ffn1[l], bf(ffn1_w1[l]), bf(ffn1_w3[l]), bf(ffn1_w2[l]), norm_final, False)
        p_hy, p_rw, p_gdn = _inproj(xt, norm_mix[l], bf(w_in_p[l]))
        y_hy = _hyena(p_hy.reshape(B, L, -1), z_pos, t_pos, prm, l)
        r, v, kk, gate, bonus, lwf, kdf, af, lwb, kdb, ab = _rw_prep(p_rw.reshape(B, L, -1), prm, l)
        yf, yb = _rw_scan(r, v, kk, lwf, kdf, af, lwb, kdb, ab)
        q, k, vg, gs = _gdn_prep(p_gdn.reshape(B, L, -1), prm, l)
        of, ob = _gdn_scan(q, k, vg, gs)
        flat = lambda a: a.reshape(T, -1)
        xt = _mix_out(xt, y_hy, flat(yf), flat(yb), flat(gate), flat(bonus), flat(of), flat(ob), p_gdn,
                      prm, l, bf(w_out[l]))
        km, vm = _mem_kv(mem, mem_norm, bf(xa_wk[l]), bf(xa_wv[l]))
        xt = _xattn(xt.reshape(B, L, D), norm_xattn[l], bf(xa_wq[l]), km, vm, bf(xa_wo[l])).reshape(T, D)
        xt = _ffn(xt, norm_ffn2[l], bf(ffn2_w1[l]), bf(ffn2_w3[l]), bf(ffn2_w2[l]), norm_final,
                  l == depth - 1)
    return xt.reshape(B, L, D)
```

```python
import functools
import math

import numpy as np
import jax
import jax.numpy as jnp
from jax import lax
from jax.experimental import pallas as pl
from jax.experimental.pallas import tpu as pltpu

F32 = jnp.float32
BF16 = jnp.bfloat16

D_MODEL = 1024
D_HYENA = 256
RW_HEADS = 6
RW_HEAD_DIM = 64
D_RWKV = RW_HEADS * RW_HEAD_DIM
GDN_HEADS = 6
GDN_HEAD_DIM = 64
D_GDN = GDN_HEADS * GDN_HEAD_DIM
HY_BANDS = 16
HY_EMB = 1 + 2 * HY_BANDS
HY_FFN = 64
RW_LORA_W = 64
RW_LORA_A = 64
RW_LORA_G = 128
RW_DECAY_SCALE = 0.606531
RW_GN_EPS = 64e-5
XA_HEADS = 4
XA_HEAD_DIM = D_MODEL // XA_HEADS
D_FF = 2816
NORM_EPS = 1e-6
L2_EPS = 1e-6
HY_COLS = 3 * D_HYENA
RW_COLS = 3 * D_RWKV + 2 * RW_LORA_W + 2 * RW_LORA_A + RW_LORA_G
GDN_COLS = 4 * D_GDN + 4 * GDN_HEADS

LANES = 128
SUBLANES = 8
MXU_WIDTH = 256
VMEM_LIMIT_BYTES = 56 * 1024 * 1024
GDN_COLS_PAD = 4 * D_GDN + LANES
CHUNK = 64
CHUNKS_PER_STEP = 4
HEAD_DIM = 64
PAIR = 2 * HEAD_DIM
FFT_NB = LANES


def _cparams(*sem):
    return pltpu.CompilerParams(dimension_semantics=sem, vmem_limit_bytes=VMEM_LIMIT_BYTES)


def _full(shape):
    nd = len(shape)
    return pl.BlockSpec(shape, lambda *_: (0,) * nd)


NN = (((1,), (0,)), ((), ()))
NT = (((1,), (1,)), ((), ()))
TN = (((0,), (0,)), ((), ()))


def _dg(a, b, dims):
    return lax.dot_general(a, b, dims, preferred_element_type=F32)


def _split2(x):
    hi = x.astype(BF16)
    lo = (x - hi.astype(F32)).astype(BF16)
    return hi, lo


def _split3(x):
    hi = x.astype(BF16)
    r = x - hi.astype(F32)
    mid = r.astype(BF16)
    lo = (r - mid.astype(F32)).astype(BF16)
    return hi, mid, lo


def _mm(a, b, dims=NN, passes=1):
    if passes == 1:
        return _dg(a.astype(BF16), b.astype(BF16), dims)
    ah, al = _split2(a)
    bh, bl = _split2(b)
    return _dg(ah, bh, dims) + (_dg(al, bh, dims) + _dg(ah, bl, dims))


def _mm_lconst(c, x):
    n = x.shape[1]
    y = _dg(c.astype(BF16), jnp.concatenate(_split3(x), axis=1), NN)
    return y[:, :n] + (y[:, n:2 * n] + y[:, 2 * n:])


def _mm_rconst(x, c):
    m = x.shape[0]
    y = _dg(jnp.concatenate(_split3(x), axis=0), c.astype(BF16), NN)
    return y[:m] + (y[m:2 * m] + y[2 * m:])


def _mm_dft(m, x):
    return _dg(m, x.astype(BF16), NN)


def _rms(x, g):
    return x * lax.rsqrt(jnp.mean(x * x, axis=-1, keepdims=True) + NORM_EPS) * g


def _sigmoid(x):
    return 1.0 / (1.0 + jnp.exp(-x))


def _silu(x):
    return x * _sigmoid(x)


def _softplus(x):
    return jnp.maximum(x, 0.0) + jnp.log(1.0 + jnp.exp(-jnp.abs(x)))


def _shift_rows(p, prev_row, next_row):
    n = p.shape[0]
    rows = lax.broadcasted_iota(jnp.int32, p.shape, 0)
    prev = jnp.where(rows == 0, prev_row, pltpu.roll(p, 1, 0))
    nxt = jnp.where(rows == n - 1, next_row, pltpu.roll(p, n - 1, 0))
    return prev, nxt


def _halo_specs(tm, width, L, col_block=0):
    r = tm // SUBLANES
    last = L // SUBLANES - 1

    def cur(b, i):
        return (b, i, col_block)

    def prev(b, i):
        return (b, jnp.maximum(i * r - 1, 0), col_block)

    def nxt(b, i):
        return (b, jnp.minimum((i + 1) * r, last), col_block)

    return (pl.BlockSpec((1, tm, width), cur),
            pl.BlockSpec((1, SUBLANES, width), prev),
            pl.BlockSpec((1, SUBLANES, width), nxt))


def _halo_rows(prev_ref, next_ref):
    i = pl.program_id(1)
    n = pl.num_programs(1)
    prev_row = jnp.where(i > 0, prev_ref[0, SUBLANES - 1:SUBLANES, :], 0.0)
    next_row = jnp.where(i < n - 1, next_ref[0, 0:1, :], 0.0)
    return prev_row, next_row


def _head_sum_matrix(width, head_dim, scale=1.0):
    idx = np.arange(width) // head_dim
    return jnp.asarray((idx[:, None] == idx[None, :]).astype(np.float32) * scale)


def _ffn_body(x_ref, g_ref, w1_ref, w3_ref, w2_ref, gf_ref, o_ref, acc_ref, *, n_chunks, tf, final):
    x = x_ref[...]
    h = _rms(x, g_ref[...]).astype(BF16)
    for j in range(n_chunks):
        sl = slice(j * tf, (j + 1) * tf)
        a = _dg(h, w1_ref[:, sl], NN)
        b = _dg(h, w3_ref[:, sl], NN)
        t = (_silu(a) * b).astype(BF16)
        part = _dg(t, w2_ref[sl, :], NN)
        if j == 0:
            acc_ref[...] = part
        else:
            acc_ref[...] += part
    y = x + 0.5 * acc_ref[...]
    if final:
        y = _rms(y, gf_ref[...])
    o_ref[...] = y


def _ffn(x, g, w1, w3, w2, gf, final):
    T, D = x.shape
    FF = w1.shape[1]
    tm = min(512, T)
    tf = MXU_WIDTH
    body = functools.partial(_ffn_body, n_chunks=FF // tf, tf=tf, final=final)
    return pl.pallas_call(
        body,
        out_shape=jax.ShapeDtypeStruct((T, D), F32),
        grid=(T // tm,),
        in_specs=[pl.BlockSpec((tm, D), lambda i: (i, 0)), _full((1, D)),
                  _full((D, FF)), _full((D, FF)), _full((FF, D)), _full((1, D))],
        out_specs=pl.BlockSpec((tm, D), lambda i: (i, 0)),
        scratch_shapes=[pltpu.VMEM((tm, D), F32)],
        compiler_params=_cparams("parallel"),
        name="ffn_final" if final else "ffn",
    )(x, g.reshape(1, D), w1, w3, w2, gf.reshape(1, D))


def _inproj_body(x_ref, g_ref, w_ref, ohy_ref, orw_ref, ogd_ref):
    h = _rms(x_ref[...], g_ref[...]).astype(BF16)
    c0, c1 = HY_COLS, HY_COLS + RW_COLS
    ohy_ref[...] = _dg(h, w_ref[:, 0:c0], NN)
    orw_ref[...] = _dg(h, w_ref[:, c0:c1], NN)
    ogd_ref[...] = _dg(h, w_ref[:, c1:c1 + GDN_COLS_PAD], NN)


def _inproj(x, g, w):
    T, D = x.shape
    tm = min(256, T)
    W = w.shape[1]
    widths = (HY_COLS, RW_COLS, GDN_COLS_PAD)
    return pl.pallas_call(
        _inproj_body,
        out_shape=tuple(jax.ShapeDtypeStruct((T, n), F32) for n in widths),
        grid=(T // tm,),
        in_specs=[pl.BlockSpec((tm, D), lambda i: (i, 0)), _full((1, D)), _full((D, W))],
        out_specs=tuple(pl.BlockSpec((tm, n), lambda i: (i, 0)) for n in widths),
        compiler_params=_cparams("parallel"),
        name="inproj",
    )(x, g.reshape(1, D), w)


def _hy_filter_body(z_ref, t_ref, freq_ref, w1_ref, b1_ref, w2_ref, b2_ref, w3_ref, dec_ref,
                    h_ref, s_ref):
    i = pl.program_id(0)
    freq = freq_ref[...]
    h = jnp.sin(freq * (_mm(z_ref[...], w1_ref[...], passes=3) + b1_ref[...]))
    h = jnp.sin(freq * (_mm(h, w2_ref[...], passes=3) + b2_ref[...]))
    h = _mm(h, w3_ref[...], passes=3) * jnp.exp(-t_ref[...] * dec_ref[...])
    C = D_HYENA
    fwd = h[:, :C]
    bwd = h[:, C:]
    rows = lax.broadcasted_iota(jnp.int32, bwd.shape, 0)
    bwd = jnp.where((rows == 0) & (i == 0), 0.0, bwd)
    h_ref[0] = fwd
    h_ref[1] = bwd
    part = jnp.sum(jnp.abs(fwd) + jnp.abs(bwd), axis=0, keepdims=True)

    @pl.when(i == 0)
    def _():
        s_ref[...] = part

    @pl.when(i > 0)
    def _():
        s_ref[...] += part


def _hy_filter(z, t, freq, w1, b1, w2, b2, w3, decay):
    L = z.shape[0]
    tl = min(1024, L)
    C = D_HYENA
    return pl.pallas_call(
        _hy_filter_body,
        out_shape=(jax.ShapeDtypeStruct((2, L, C), F32), jax.ShapeDtypeStruct((1, C), F32)),
        grid=(L // tl,),
        in_specs=[pl.BlockSpec((tl, LANES), lambda i: (i, 0)), pl.BlockSpec((tl, 1), lambda i: (i, 0)),
                  _full((1, HY_FFN)), _full((LANES, HY_FFN)), _full((1, HY_FFN)),
                  _full((HY_FFN, HY_FFN)), _full((1, HY_FFN)), _full((HY_FFN, 2 * C)), _full((1, 2 * C))],
        out_specs=(pl.BlockSpec((2, tl, C), lambda i: (0, i, 0)), _full((1, C))),
        compiler_params=_cparams("arbitrary"),
        name="hy_filter",
    )(z, t, freq.reshape(1, -1), w1, b1.reshape(1, -1), w2, b2.reshape(1, -1), w3, decay.reshape(1, -1))


def _hy_pre_body(p_ref, pp_ref, pn_ref, w_ref, b_ref, u_ref, x0_ref):
    p = p_ref[0]
    prev_row, next_row = _halo_rows(pp_ref, pn_ref)
    prev, nxt = _shift_rows(p, prev_row, next_row)
    c = w_ref[0:1, :] * prev + w_ref[1:2, :] * p + w_ref[2:3, :] * nxt + b_ref[...]
    C = D_HYENA
    x0_ref[0] = c[:, :C]
    u_ref[0] = c[:, C:2 * C] * c[:, 2 * C:]


def _hy_pre(p, w, b):
    B, L, W = p.shape
    tm = min(512, L)
    C = D_HYENA
    return pl.pallas_call(
        _hy_pre_body,
        out_shape=(jax.ShapeDtypeStruct((B, L, C), F32),) * 2,
        grid=(B, L // tm),
        in_specs=[*_halo_specs(tm, W, L), _full((3, W)), _full((1, W))],
        out_specs=(pl.BlockSpec((1, tm, C), lambda b, i: (b, i, 0)),) * 2,
        compiler_params=_cparams("parallel", "parallel"),
        name="hy_pre",
    )(p, p, p, w, b.reshape(1, W))


def _fft_a_body(u_ref, m_ref, twr_ref, twi_ref, o_ref, *, na):
    a = _mm_dft(m_ref[...], u_ref[0])
    ar, ai = a[:na], a[na:]
    twr, twi = twr_ref[...], twi_ref[...]
    o_ref[0, 0] = (ar * twr - ai * twi).astype(BF16)
    o_ref[0, 1] = (ar * twi + ai * twr).astype(BF16)


def _fft_a(u2d, m, twr, twi):
    B, half, cols = u2d.shape
    na = 2 * half
    tn = min(2048, cols)
    return pl.pallas_call(
        functools.partial(_fft_a_body, na=na),
        out_shape=jax.ShapeDtypeStruct((B, 2, na, cols), BF16),
        grid=(B, cols // tn),
        in_specs=[pl.BlockSpec((1, half, tn), lambda b, j: (b, 0, j)), _full((2 * na, half)),
                  pl.BlockSpec((na, tn), lambda b, j: (0, j)), pl.BlockSpec((na, tn), lambda b, j: (0, j))],
        out_specs=pl.BlockSpec((1, 2, na, tn), lambda b, j: (b, 0, 0, j)),
        compiler_params=_cparams("parallel", "parallel"),
        name="fft_a",
    )(u2d, m, twr, twi)


def _fft_filt_body(bf_ref, bb_ref, m_ref, s_ref, o_ref, *, tk):
    nb = FFT_NB
    inv = 1.0 / s_ref[...]
    for k in range(tk):
        xf = _mm_dft(m_ref[...], jnp.concatenate([bf_ref[0, 0, k], bf_ref[0, 1, k]], axis=0))
        xb = _mm_dft(m_ref[...], jnp.concatenate([bb_ref[0, 0, k], bb_ref[0, 1, k]], axis=0))
        o_ref[0, k] = (xf[:nb] + xb[:nb]) * inv
        o_ref[1, k] = (xf[nb:] - xb[nb:]) * inv


def _fft_filt(bt, m, s):
    _, _, na, nb, C = bt.shape
    tk = min(4, na)
    return pl.pallas_call(
        functools.partial(_fft_filt_body, tk=tk),
        out_shape=jax.ShapeDtypeStruct((2, na, nb, C), F32),
        grid=(na // tk,),
        in_specs=[pl.BlockSpec((1, 2, tk, nb, C), lambda i: (0, 0, i, 0, 0)),
                  pl.BlockSpec((1, 2, tk, nb, C), lambda i: (1, 0, i, 0, 0)),
                  _full((2 * nb, 2 * nb)), _full((1, C))],
        out_specs=pl.BlockSpec((2, tk, nb, C), lambda i: (0, i, 0, 0)),
        compiler_params=_cparams("parallel"),
        name="fft_filt",
    )(bt, bt, m, s)


def _fft_c_body(b_ref, k_ref, mf_ref, mi_ref, twr_ref, twi_ref, o_ref, *, tk):
    nb = FFT_NB
    for k in range(tk):
        x = _mm_dft(mf_ref[...], jnp.concatenate([b_ref[0, 0, k], b_ref[0, 1, k]], axis=0))
        xr, xi = x[:nb], x[nb:]
        kr, ki = k_ref[0, k], k_ref[1, k]
        y = jnp.concatenate([xr * kr - xi * ki, xr * ki + xi * kr], axis=0)
        d = _mm_dft(mi_ref[...], y)
        dr, di = d[:nb], d[nb:]
        reps = dr.shape[1] // LANES
        twr = jnp.concatenate([twr_ref[k]] * reps, axis=1)
        twi = jnp.concatenate([twi_ref[k]] * reps, axis=1)
        o_ref[0, 0, k] = (dr * twr + di * twi).astype(BF16)
        o_ref[0, 1, k] = (di * twr - dr * twi).astype(BF16)


def _fft_c(bt, khat, mf, mi, twr, twi):
    B, _, na, nb, C = bt.shape
    tk = min(4, na)
    blk = pl.BlockSpec((1, 2, tk, nb, C), lambda b, i: (b, 0, i, 0, 0))
    m = _full((2 * nb, 2 * nb))
    tw = pl.BlockSpec((tk, nb, LANES), lambda b, i: (i, 0, 0))
    return pl.pallas_call(
        functools.partial(_fft_c_body, tk=tk),
        out_shape=jax.ShapeDtypeStruct(bt.shape, BF16),
        grid=(B, na // tk),
        in_specs=[blk, pl.BlockSpec((2, tk, nb, C), lambda b, i: (0, i, 0, 0)), m, m, tw, tw],
        out_specs=blk,
        compiler_params=_cparams("parallel", "parallel"),
        name="fft_c",
    )(bt, khat, mf, mi, twr, twi)


def _fft_out_body(e_ref, m_ref, u_ref, x0_ref, bias_ref, o_ref):
    conv = _mm_dft(m_ref[...], e_ref[0])
    u = u_ref[0]
    o_ref[0] = (x0_ref[0] * (conv + bias_ref[...] * u)).astype(BF16)


def _fft_out(e2d, m, u2d, x02d, bias_t):
    B, half, cols = u2d.shape
    tn = min(2048, cols)
    blk = pl.BlockSpec((1, half, tn), lambda b, j: (b, 0, j))
    return pl.pallas_call(
        _fft_out_body,
        out_shape=jax.ShapeDtypeStruct(u2d.shape, BF16),
        grid=(B, cols // tn),
        in_specs=[pl.BlockSpec((1, 4 * half, tn), lambda b, j: (b, 0, j)), _full((half, 4 * half)), blk, blk,
                  pl.BlockSpec((1, tn), lambda b, j: (0, j))],
        out_specs=blk,
        compiler_params=_cparams("parallel", "parallel"),
        name="fft_out",
    )(e2d, m, u2d, x02d, bias_t)


@functools.lru_cache(maxsize=None)
def _fft_consts_np(L):
    n = 2 * L
    nb = FFT_NB
    na = n // nb
    half = na // 2
    ka = np.arange(na)
    fa = np.exp(-2j * np.pi * np.outer(ka, ka) / na)
    kb = np.arange(nb)
    fb = np.exp(-2j * np.pi * np.outer(kb, kb) / nb)
    m_a = np.concatenate([fa.real[:, :half], fa.imag[:, :half]], axis=0)
    m_f = np.block([[fb.real, -fb.imag], [fb.imag, fb.real]])
    m_i = np.block([[fb.real, fb.imag], [-fb.imag, fb.real]])
    m_o = np.concatenate([fa.real[:half, :], fa.imag[:half, :]], axis=1) / n
    tw = np.exp(-2j * np.pi * (np.outer(ka, kb) % n) / n)
    f32 = lambda a: np.asarray(a, np.float32)
    return f32(m_a), f32(m_f), f32(m_i), f32(m_o), f32(tw.real), f32(tw.imag)


def _hyena_pos(L):
    t = jnp.linspace(0.0, 1.0, L, dtype=F32)[:, None]
    ang = 2.0 * math.pi * jnp.arange(L, dtype=F32)[:, None] / L
    bands = jnp.linspace(1e-4, HY_BANDS - 1, HY_BANDS, dtype=F32)[None, :]
    z = jnp.concatenate([t, jnp.cos(bands * ang), -jnp.sin(bands * ang)], axis=-1)
    return jnp.pad(z, ((0, 0), (0, LANES - HY_EMB))), t


def _hyena(p_hy, z, t, prm, l):
    B, L, _ = p_hy.shape
    C = D_HYENA
    nb = FFT_NB
    na = 2 * L // nb
    half = na // 2
    m_a, m_f, m_i, m_o, twr, twi = _fft_consts_np(L)
    ma, mf, mi, mo = (jnp.asarray(m, BF16) for m in (m_a, m_f, m_i, m_o))
    twr_cols = jnp.repeat(jnp.asarray(twr), C, axis=1)
    twi_cols = jnp.repeat(jnp.asarray(twi), C, axis=1)
    twr_l = jnp.broadcast_to(jnp.asarray(twr)[:, :, None], (na, nb, LANES))
    twi_l = jnp.broadcast_to(jnp.asarray(twi)[:, :, None], (na, nb, LANES))

    w1 = jnp.pad(prm["hy_w1"][l], ((0, LANES - HY_EMB), (0, 0)))
    hraw, hsum = _hy_filter(z, t, prm["hy_freq"][l], w1, prm["hy_b1"][l], prm["hy_w2"][l],
                            prm["hy_b2"][l], prm["hy_w3"][l], prm["hy_decay"][l])
    hb = _fft_a(hraw.reshape(2, half, nb * C), ma, twr_cols, twi_cols)
    khat = _fft_filt(hb.reshape(2, 2, na, nb, C), mf, hsum)

    u, x0 = _hy_pre(p_hy, prm["hy_conv_w"][l], prm["hy_conv_b"][l])
    u2d = u.reshape(B, half, nb * C)
    bt = _fft_a(u2d, ma, twr_cols, twi_cols)
    e = _fft_c(bt.reshape(B, 2, na, nb, C), khat, mf, mi, twr_l, twi_l)
    bias_t = jnp.tile(prm["hy_bias"][l], nb).reshape(1, nb * C)
    y = _fft_out(e.reshape(B, 2 * na, nb * C), mo, u2d, x0.reshape(B, half, nb * C), bias_t)
    return y.reshape(B * L, C)


def _head_block_mask():
    r = lax.broadcasted_iota(jnp.int32, (PAIR, PAIR), 0)
    c = lax.broadcasted_iota(jnp.int32, (PAIR, PAIR), 1)
    return ((r // HEAD_DIM) == (c // HEAD_DIM)).astype(F32)


def _bd(x, hm):
    return jnp.concatenate([x, x], axis=0) * hm


def _cat_masks(reverse):
    r = lax.broadcasted_iota(jnp.int32, (CHUNK, PAIR), 0)
    s = lax.broadcasted_iota(jnp.int32, (CHUNK, PAIR), 1) % CHUNK
    if reverse:
        return s > r, s >= r, s == r
    return s < r, s <= r, s == r


NEUMANN_PASSES = 1
GRAM_PASSES = 1


def _neumann_inverse(a, diag, hm):
    C = CHUNK
    t = jnp.where(diag, 1.0, 0.0) + a
    p = _mm(a, _bd(a, hm), passes=NEUMANN_PASSES)
    yield
    steps = int(math.log2(C)) - 1
    for _ in range(steps - 1):
        pt = _mm(jnp.concatenate([p, t], axis=0), _bd(p, hm), passes=NEUMANN_PASSES)
        yield
        p, t = pt[:C], t + pt[C:]
    return t + _mm(t, _bd(p, hm), passes=NEUMANN_PASSES)


def _interleave(gens):
    results = [None] * len(gens)
    alive = list(range(len(gens)))
    while alive:
        for i in list(alive):
            try:
                next(gens[i])
            except StopIteration as done:
                results[i] = done.value
                alive.remove(i)
    return results


def _rw_prep_body(p_ref, pp_ref, pn_ref, mup_ref, mun_ref, wl_ref, w0_ref, al_ref, a0_ref, gl_ref,
                  kk_ref, ka_ref, rk_ref, hs_ref,
                  v_o, gate_o, bonus_o, atf_o, rtf_o, khf_o, bhf_o, atb_o, rtb_o, khb_o, bhb_o, ptf_o, ptb_o):
    p = p_ref[0]
    tm = p.shape[0]
    prev_row, next_row = _halo_rows(pp_ref, pn_ref)
    prev, nxt = _shift_rows(p, prev_row, next_row)
    p = p + mup_ref[...] * (prev - p) + mun_ref[...] * (nxt - p)
    D = D_RWKV
    r, k, v = p[:, 0:D], p[:, D:2 * D], p[:, 2 * D:3 * D]
    o = 3 * D
    lw = (p[:, o:o + RW_LORA_W], p[:, o + RW_LORA_W:o + 2 * RW_LORA_W])
    o += 2 * RW_LORA_W
    la = (p[:, o:o + RW_LORA_A], p[:, o + RW_LORA_A:o + 2 * RW_LORA_A])
    o += 2 * RW_LORA_A
    lg = p[:, o:o + RW_LORA_G]
    hs = hs_ref[...]
    gate_o[0] = _mm(_sigmoid(lg), gl_ref[...], passes=3)
    k2 = k * kk_ref[...]
    kkn = k2 * lax.rsqrt(_mm_rconst(k2 * k2, hs) + L2_EPS)
    v_o[0] = v.astype(BF16)
    bonus_o[0] = _mm_rconst(r * k * rk_ref[...], hs) * v
    outs = ((atf_o, rtf_o, khf_o, bhf_o, ptf_o), (atb_o, rtb_o, khb_o, bhb_o, ptb_o))
    tot_sel = _chunk_total_matrix(tm)
    for d in range(2):
        at_o, rt_o, kh_o, bh_o, pt_o = outs[d]
        logw = -RW_DECAY_SCALE * _sigmoid(w0_ref[d:d + 1, :] + _mm(jnp.tanh(lw[d]), wl_ref[d], passes=3))
        a = _sigmoid(a0_ref[d:d + 1, :] + _mm(la[d], al_ref[d], passes=3))
        kd = k * (1.0 + (a - 1.0) * ka_ref[...])
        cum = _mm_lconst(_chunk_cum_matrix(tm, bool(d)), logw)
        pinv = jnp.exp(-cum)
        at_o[0] = (-kkn * jnp.exp(cum - logw)).astype(BF16)
        rt_o[0] = (r * jnp.exp(cum)).astype(BF16)
        kh_o[0] = (kd * pinv).astype(BF16)
        bh_o[0] = (kkn * a * pinv).astype(BF16)
        pt_o[0] = jnp.exp(_mm_lconst(tot_sel, logw))


def _chunk_cum_matrix(tm, reverse):
    r = lax.broadcasted_iota(jnp.int32, (tm, tm), 0)
    c = lax.broadcasted_iota(jnp.int32, (tm, tm), 1)
    same = (r // CHUNK) == (c // CHUNK)
    return (same & ((c >= r) if reverse else (c <= r))).astype(F32)


def _chunk_total_matrix(tm):
    r = lax.broadcasted_iota(jnp.int32, (tm // SUBLANES, tm), 0)
    c = lax.broadcasted_iota(jnp.int32, (tm // SUBLANES, tm), 1)
    return (r // (CHUNK // SUBLANES) == c // CHUNK).astype(F32)


def _rw_prep(p, prm, l):
    B, L, W = p.shape
    tm = min(256, L)
    D = D_RWKV
    row = lambda a: a.reshape(1, -1)
    out = pl.BlockSpec((1, tm, D), lambda b, i: (b, i, 0))
    tot = pl.BlockSpec((1, tm // SUBLANES, D), lambda b, i: (b, i, 0))
    full = jax.ShapeDtypeStruct((B, L, D), F32)
    half = jax.ShapeDtypeStruct((B, L, D), BF16)
    small = jax.ShapeDtypeStruct((B, L // SUBLANES, D), F32)
    return pl.pallas_call(
        _rw_prep_body,
        out_shape=(half, full, full) + (half,) * 8 + (small,) * 2,
        grid=(B, L // tm),
        in_specs=[*_halo_specs(tm, W, L), _full((1, W)), _full((1, W)),
                  _full((2, RW_LORA_W, D)), _full((2, D)), _full((2, RW_LORA_A, D)), _full((2, D)),
                  _full((RW_LORA_G, D)), _full((1, D)), _full((1, D)), _full((1, D)), _full((D, D))],
        out_specs=(out,) * 11 + (tot,) * 2,
        compiler_params=_cparams("parallel", "parallel"),
        name="rw_prep",
    )(p, p, p, row(prm["rw_mu_prev"][l]), row(prm["rw_mu_next"][l]), prm["rw_w_lora"][l], prm["rw_w0"][l],
      prm["rw_a_lora"][l], prm["rw_a0"][l], prm["rw_g_lora"][l], row(prm["rw_k_k"][l]),
      row(prm["rw_k_a"][l]), row(prm["rw_r_k"][l]), _head_sum_matrix(D, RW_HEAD_DIM))


def _rw_chunk(v, at, rt, kh, bh, ptot, s_in, s_out, reverse, hm):
    C = CHUNK
    strict, incl, diag = _cat_masks(reverse)
    kb_bd = jnp.concatenate([_bd(kh, hm), _bd(bh, hm)], axis=0)
    g = _mm(jnp.concatenate([at, rt], axis=0), kb_bd, NT, passes=GRAM_PASSES)
    yield
    n = 2 * C
    a_ak = jnp.where(strict, g[:C, :n], 0.0)
    a_ab = jnp.where(strict, g[:C, n:], 0.0)
    a_rk = jnp.where(incl, g[C:, :n], 0.0)
    a_rb = jnp.where(incl, g[C:, n:], 0.0)
    av = _mm(jnp.concatenate([a_ak, a_rk], axis=0), _bd(v, hm))
    t = yield from _neumann_inverse(a_ab, diag, hm)
    yield
    uw = _mm(t, jnp.concatenate([_bd(av[:C], hm), _bd(at, hm)], axis=1))
    yield
    while s_in[0] is None:
        yield
    s0 = s_in[0]
    ws = _mm(jnp.concatenate([uw[:, PAIR:], rt], axis=0), s0, NT)
    yield
    u = uw[:, :PAIR] + ws[:C]
    upd = _mm(jnp.concatenate([v, u], axis=0), jnp.concatenate([kh, bh], axis=0), TN) * hm
    s_out[0] = (s0 + upd) * ptot
    return ws[C:] + av[C:] + _mm(a_rb, _bd(u, hm))


def _scan_schedule(n_pairs, make_chain, s_ref):
    gens, outs, finals = [], [], []
    for c in range(CHUNKS_PER_STEP):
        for d in range(2):
            ci = CHUNKS_PER_STEP - 1 - c if d else c
            rows = slice(ci * CHUNK, (ci + 1) * CHUNK)
            for j in range(n_pairs):
                if c == 0:
                    finals.append([s_ref[d, j]])
                k = d * n_pairs + j
                s_in = finals[k]
                s_out = [None]
                finals[k] = s_out
                gens.append(make_chain(d, j, rows, s_in, s_out))
                outs.append((d, j, rows))
    return gens, outs, finals


def _rw_scan_body(vf, atf, rtf, khf, bhf, ptf, vb, atb, rtb, khb, bhb, ptb, yf_o, yb_o, s_ref):
    @pl.when(pl.program_id(1) == 0)
    def _():
        s_ref[...] = jnp.zeros(s_ref.shape, F32)

    hm = _head_block_mask()
    dirs = ((vf, atf, rtf, khf, bhf, ptf, yf_o), (vb, atb, rtb, khb, bhb, ptb, yb_o))
    n_pairs = D_RWKV // PAIR

    def make_chain(d, j, rows, s_in, s_out):
        sl = slice(j * PAIR, (j + 1) * PAIR)
        v, at, rt, kh, bh, pt, _ = dirs[d]
        t0 = rows.start // SUBLANES
        return _rw_chunk(v[0, rows, sl], at[0, rows, sl], rt[0, rows, sl], kh[0, rows, sl], bh[0, rows, sl],
                         pt[0, t0:t0 + 1, sl], s_in, s_out, bool(d), hm)

    gens, outs, finals = _scan_schedule(n_pairs, make_chain, s_ref)
    for y, (d, j, rows) in zip(_interleave(gens), outs):
        dirs[d][-1][0, rows, j * PAIR:(j + 1) * PAIR] = y
    for k, cell in enumerate(finals):
        s_ref[k // n_pairs, k % n_pairs] = cell[0]


def _rw_scan(v, atf, rtf, khf, bhf, ptf, atb, rtb, khb, bhb, ptb):
    B, L, D = v.shape
    blk = CHUNKS_PER_STEP * CHUNK
    n = L // blk
    fw = pl.BlockSpec((1, blk, D), lambda b, i: (b, i, 0))
    bw = pl.BlockSpec((1, blk, D), lambda b, i: (b, n - 1 - i, 0))
    fwt = pl.BlockSpec((1, blk // SUBLANES, D), lambda b, i: (b, i, 0))
    bwt = pl.BlockSpec((1, blk // SUBLANES, D), lambda b, i: (b, n - 1 - i, 0))
    return pl.pallas_call(
        _rw_scan_body,
        out_shape=(jax.ShapeDtypeStruct((B, L, D), F32),) * 2,
        grid=(B, n),
        in_specs=[fw] * 5 + [fwt] + [bw] * 5 + [bwt],
        out_specs=(fw, bw),
        scratch_shapes=[pltpu.VMEM((2, D // PAIR, PAIR, PAIR), F32)],
        compiler_params=_cparams("parallel", "arbitrary"),
        name="rw_scan",
    )(v, atf, rtf, khf, bhf, ptf, v, atb, rtb, khb, bhb, ptb)


GDN_G_LANE = 0
GDN_BETA_LANE = 2 * GDN_HEADS


def _gdn_prep_body(p_ref, pp_ref, pn_ref, s_ref, w_ref, alog_ref, dt_ref, hs_ref,
                   q_o, k_o, v_o, gcf_o, gcb_o, bf_o, bb_o):
    p = p_ref[0]
    tm = p.shape[0]
    prev_row, next_row = _halo_rows(pp_ref, pn_ref)
    prev, nxt = _shift_rows(p, prev_row, next_row)
    c = _silu(w_ref[0:1, :] * prev + w_ref[1:2, :] * p + w_ref[2:3, :] * nxt)
    D = D_GDN
    q, k, v = c[:, 0:D], c[:, D:2 * D], c[:, 2 * D:3 * D]
    hs = hs_ref[...]
    q_o[0] = q * lax.rsqrt(_mm_rconst(q * q, hs) + L2_EPS) * (GDN_HEAD_DIM ** -0.5)
    k_o[0] = k * lax.rsqrt(_mm_rconst(k * k, hs) + L2_EPS)
    v_o[0] = v
    s = s_ref[0]
    lane = lax.broadcasted_iota(jnp.int32, s.shape, 1)
    g = -jnp.exp(alog_ref[...]) * _softplus(s + dt_ref[...])
    gs = jnp.where(lane < GDN_BETA_LANE, g, jnp.where(lane < 4 * GDN_HEADS, _sigmoid(s), 0.0))
    er = lax.broadcasted_iota(jnp.int32, (LANES, 4 * D), 0)
    ec = lax.broadcasted_iota(jnp.int32, (LANES, 4 * D), 1)
    gx = _mm_rconst(gs, (er == (ec // D) * GDN_HEADS + (ec % D) // GDN_HEAD_DIM).astype(F32))
    gcf_o[0] = _mm_lconst(_chunk_cum_matrix(tm, False), gx[:, 0:D])
    gcb_o[0] = _mm_lconst(_chunk_cum_matrix(tm, True), gx[:, D:2 * D])
    bf_o[0] = gx[:, 2 * D:3 * D]
    bb_o[0] = gx[:, 3 * D:]


def _gdn_prep(p, prm, l):
    B, L, W = p.shape
    tm = min(256, L)
    D = D_GDN
    Wq = 3 * D
    pad = LANES - 2 * GDN_HEADS
    alog = jnp.pad(prm["gdn_a_log"][l].reshape(-1), (0, pad)).reshape(1, LANES)
    dt = jnp.pad(prm["gdn_dt_bias"][l].reshape(-1), (0, pad)).reshape(1, LANES)
    out = pl.BlockSpec((1, tm, D), lambda b, i: (b, i, 0))
    side = pl.BlockSpec((1, tm, LANES), lambda b, i: (b, i, 4 * D // LANES))
    return pl.pallas_call(
        _gdn_prep_body,
        out_shape=(jax.ShapeDtypeStruct((B, L, D), F32),) * 7,
        grid=(B, L // tm),
        in_specs=[*_halo_specs(tm, Wq, L), side, _full((3, Wq)), _full((1, LANES)), _full((1, LANES)),
                  _full((D, D))],
        out_specs=(out,) * 7,
        compiler_params=_cparams("parallel", "parallel"),
        name="gdn_prep",
    )(p, p, p, p, prm["gdn_conv_w"][l], alog, dt, _head_sum_matrix(D, GDN_HEAD_DIM))


def _gdn_chunk(q, k, v, gcx, beta, s_in, s_out, reverse, hm):
    C = CHUNK
    strict, incl, diag = _cat_masks(reverse)
    grow = _mm_lconst(jnp.ones((C, C), F32), jnp.where(diag, gcx, 0.0))
    kb = k * beta
    kq = _mm(jnp.concatenate([kb, q], axis=0), _bd(k, hm), NT, passes=GRAM_PASSES)
    yield
    dec = jnp.exp(jnp.where(incl, gcx - grow, -1e30))
    a = jnp.where(strict, kq[:C] * dec, 0.0)
    attn = kq[C:] * dec
    t = yield from _neumann_inverse(-a, diag, hm)
    yield
    egc = jnp.exp(gcx)
    sol = _mm(t, jnp.concatenate([_bd(v * beta, hm), _bd(kb * egc, hm)], axis=1))
    yield
    while s_in[0] is None:
        yield
    s0 = s_in[0]
    ws = _mm(jnp.concatenate([sol[:, PAIR:], q * egc], axis=0), s0)
    yield
    v_new = sol[:, :PAIR] - ws[:C]
    glast = gcx[0:1] if reverse else gcx[C - 1:C]
    upd = _mm(k * jnp.exp(glast - gcx), v_new, TN) * hm
    s_out[0] = s0 * jnp.exp(glast) + upd
    return ws[C:] + _mm(attn, _bd(v_new, hm))


def _gdn_scan_body(qf, kf, vf, gf, btf, qb, kb, vb, gb, btb, of_o, ob_o, s_ref):
    @pl.when(pl.program_id(1) == 0)
    def _():
        s_ref[...] = jnp.zeros(s_ref.shape, F32)

    hm = _head_block_mask()
    dirs = ((qf, kf, vf, gf, btf, of_o), (qb, kb, vb, gb, btb, ob_o))
    n_pairs = D_GDN // PAIR

    def make_chain(d, j, rows, s_in, s_out):
        sl = slice(j * PAIR, (j + 1) * PAIR)
        q, k, v, g, bt, _ = dirs[d]
        return _gdn_chunk(q[0, rows, sl], k[0, rows, sl], v[0, rows, sl], g[0, rows, sl], bt[0, rows, sl],
                          s_in, s_out, bool(d), hm)

    gens, outs, finals = _scan_schedule(n_pairs, make_chain, s_ref)
    for o, (d, j, rows) in zip(_interleave(gens), outs):
        dirs[d][-1][0, rows, j * PAIR:(j + 1) * PAIR] = o
    for k, cell in enumerate(finals):
        s_ref[k // n_pairs, k % n_pairs] = cell[0]


def _gdn_scan(q, k, v, gcf, gcb, btf, btb):
    B, L, D = q.shape
    blk = CHUNKS_PER_STEP * CHUNK
    n = L // blk
    fw = pl.BlockSpec((1, blk, D), lambda b, i: (b, i, 0))
    bw = pl.BlockSpec((1, blk, D), lambda b, i: (b, n - 1 - i, 0))
    return pl.pallas_call(
        _gdn_scan_body,
        out_shape=(jax.ShapeDtypeStruct((B, L, D), F32),) * 2,
        grid=(B, n),
        in_specs=[fw] * 5 + [bw] * 5,
        out_specs=(fw, bw),
        scratch_shapes=[pltpu.VMEM((2, D // PAIR, PAIR, PAIR), F32)],
        compiler_params=_cparams("parallel", "arbitrary"),
        name="gdn_scan",
    )(q, k, v, gcf, btf, q, k, v, gcb, btb)


def _mix_out_body(x_ref, yhy_ref, yf_ref, yb_ref, gate_ref, bonus_ref, of_ref, ob_ref, zg_ref,
                  gnw_ref, gnb_ref, nw_ref, avg_ref, w_ref, o_ref):
    avg = avg_ref[...]
    y = yf_ref[...] + yb_ref[...]
    mu = _mm_rconst(y, avg)
    dlt = y - mu
    var = _mm_rconst(dlt * dlt, avg)
    y_rw = (dlt * lax.rsqrt(var + RW_GN_EPS) * gnw_ref[...] + gnb_ref[...] + bonus_ref[...]) * gate_ref[...]
    o = of_ref[...] + ob_ref[...]
    ms = _mm_rconst(o * o, avg)
    y_gdn = o * lax.rsqrt(ms + NORM_EPS) * nw_ref[...] * _silu(zg_ref[...])
    c0, c1 = D_HYENA, D_HYENA + D_RWKV
    acc = _dg(yhy_ref[...].astype(BF16), w_ref[0:c0, :], NN)
    acc += _dg(y_rw.astype(BF16), w_ref[c0:c1, :], NN)
    acc += _dg(y_gdn.astype(BF16), w_ref[c1:, :], NN)
    o_ref[...] = x_ref[...] + acc


def _mix_out(x, y_hy, yf, yb, gate, bonus, of, ob, p_gdn, prm, l, w_out):
    T, D = x.shape
    tm = min(512, T)
    Dh = D_RWKV
    row = lambda a: a.reshape(1, -1)
    tile = lambda n: pl.BlockSpec((tm, n), lambda i: (i, 0))
    zg = pl.BlockSpec((tm, D_GDN), lambda i: (i, 3))
    avg = _head_sum_matrix(Dh, HEAD_DIM, 1.0 / HEAD_DIM)
    return pl.pallas_call(
        _mix_out_body,
        out_shape=jax.ShapeDtypeStruct((T, D), F32),
        grid=(T // tm,),
        in_specs=[tile(D), tile(D_HYENA)] + [tile(Dh)] * 6 + [zg, _full((1, Dh)), _full((1, Dh)),
                  _full((1, Dh)), _full((Dh, Dh)), _full((D, D))],
        out_specs=tile(D),
        compiler_params=_cparams("parallel"),
        name="mix_out",
    )(x, y_hy, yf, yb, gate, bonus, of, ob, p_gdn, row(prm["rw_gn_w"][l]), row(prm["rw_gn_b"][l]),
      row(jnp.tile(prm["gdn_norm_w"][l], GDN_HEADS)), avg, w_out)


def _mem_kv_body(m_ref, g_ref, wk_ref, wv_ref, k_o, v_o):
    h = _rms(m_ref[0], g_ref[...]).astype(BF16)
    k_o[0] = _dg(h, wk_ref[...], NN).astype(BF16)
    v_o[0] = _dg(h, wv_ref[...], NN).astype(BF16)


def _mem_kv(mem, g, wk, wv):
    B, M, D = mem.shape
    blk = pl.BlockSpec((1, M, D), lambda b: (b, 0, 0))
    return pl.pallas_call(
        _mem_kv_body,
        out_shape=(jax.ShapeDtypeStruct((B, M, D), BF16),) * 2,
        grid=(B,),
        in_specs=[blk, _full((1, D)), _full((D, D)), _full((D, D))],
        out_specs=(blk, blk),
        compiler_params=_cparams("parallel"),
        name="mem_kv",
    )(mem, g.reshape(1, D), wk, wv)


def _xattn_body(x_ref, g_ref, wq_ref, k_ref, v_ref, wo_ref, o_ref):
    x = x_ref[0]
    h = _rms(x, g_ref[...]).astype(BF16)
    q = (_dg(h, wq_ref[...], NN) * (XA_HEAD_DIM ** -0.5)).astype(BF16)
    outs = []
    for hd in range(XA_HEADS):
        sl = slice(hd * XA_HEAD_DIM, (hd + 1) * XA_HEAD_DIM)
        s = _dg(q[:, sl], k_ref[0, :, sl], NT)
        s = s - jnp.max(s, axis=-1, keepdims=True)
        e = jnp.exp(s)
        pr = e / jnp.sum(e, axis=-1, keepdims=True)
        outs.append(_dg(pr.astype(BF16), v_ref[0, :, sl], NN))
    o = jnp.concatenate(outs, axis=1).astype(BF16)
    o_ref[0] = x + _dg(o, wo_ref[...], NN)


def _xattn(x, g, wq, k, v, wo):
    B, L, D = x.shape
    M = k.shape[1]
    tm = min(512, L)
    tile = pl.BlockSpec((1, tm, D), lambda b, i: (b, i, 0))
    kv = pl.BlockSpec((1, M, D), lambda b, i: (b, 0, 0))
    return pl.pallas_call(
        _xattn_body,
        out_shape=jax.ShapeDtypeStruct((B, L, D), F32),
        grid=(B, L // tm),
        in_specs=[tile, _full((1, D)), _full((D, D)), kv, kv, _full((D, D))],
        out_specs=tile,
        compiler_params=_cparams("parallel", "parallel"),
        name="xattn",
    )(x, g.reshape(1, D), wq, k, v, wo)


def kernel(x, mem, norm_ffn1, ffn1_w1, ffn1_w3, ffn1_w2, norm_mix, w_in, w_out, hy_conv_w, hy_conv_b, hy_freq, hy_w1, hy_b1, hy_w2, hy_b2, hy_w3, hy_decay, hy_bias, rw_mu_prev, rw_mu_next, rw_w_lora, rw_w0, rw_a_lora, rw_a0, rw_g_lora, rw_k_k, rw_k_a, rw_r_k, rw_gn_w, rw_gn_b, gdn_conv_w, gdn_a_log, gdn_dt_bias, gdn_norm_w, norm_xattn, xa_wq, xa_wk, xa_wv, xa_wo, mem_norm, norm_ffn2, ffn2_w1, ffn2_w3, ffn2_w2, norm_final):
    prm = dict(hy_conv_w=hy_conv_w, hy_conv_b=hy_conv_b, hy_freq=hy_freq, hy_w1=hy_w1, hy_b1=hy_b1,
               hy_w2=hy_w2, hy_b2=hy_b2, hy_w3=hy_w3, hy_decay=hy_decay, hy_bias=hy_bias,
               rw_mu_prev=rw_mu_prev, rw_mu_next=rw_mu_next, rw_w_lora=rw_w_lora, rw_w0=rw_w0,
               rw_a_lora=rw_a_lora, rw_a0=rw_a0, rw_g_lora=rw_g_lora, rw_k_k=rw_k_k, rw_k_a=rw_k_a,
               rw_r_k=rw_r_k, rw_gn_w=rw_gn_w, rw_gn_b=rw_gn_b, gdn_conv_w=gdn_conv_w,
               gdn_a_log=gdn_a_log, gdn_dt_bias=gdn_dt_bias, gdn_norm_w=gdn_norm_w)
    B, L, D = x.shape
    depth = norm_ffn1.shape[0]
    T = B * L
    z_pos, t_pos = _hyena_pos(L)
    bf = lambda w: w.astype(BF16)
    w_in_p = jnp.pad(w_in, ((0, 0), (0, 0), (0, GDN_COLS_PAD - GDN_COLS)))
    xt = x.reshape(T, D)
    for l in range(depth):
        xt = _ffn(xt, norm_ffn1[l], bf(ffn1_w1[l]), bf(ffn1_w3[l]), bf(ffn1_w2[l]), norm_final, False)
        p_hy, p_rw, p_gdn = _inproj(xt, norm_mix[l], bf(w_in_p[l]))
        y_hy = _hyena(p_hy.reshape(B, L, -1), z_pos, t_pos, prm, l)
        v, gate, bonus, *rw_ops = _rw_prep(p_rw.reshape(B, L, -1), prm, l)
        yf, yb = _rw_scan(v, *rw_ops[0:4], rw_ops[8], *rw_ops[4:8], rw_ops[9])
        of, ob = _gdn_scan(*_gdn_prep(p_gdn.reshape(B, L, -1), prm, l))
        flat = lambda a: a.reshape(T, -1)
        xt = _mix_out(xt, y_hy, flat(yf), flat(yb), flat(gate), flat(bonus), flat(of), flat(ob), p_gdn,
                      prm, l, bf(w_out[l]))
        km, vm = _mem_kv(mem, mem_norm, bf(xa_wk[l]), bf(xa_wv[l]))
        xt = _xattn(xt.reshape(B, L, D), norm_xattn[l], bf(xa_wq[l]), km, vm, bf(xa_wo[l])).reshape(T, D)
        xt = _ffn(xt, norm_ffn2[l], bf(ffn2_w1[l]), bf(ffn2_w3[l]), bf(ffn2_w2[l]), norm_final,
                  l == depth - 1)
    return xt.reshape(B, L, D)
```

```python
import functools
import math

import numpy as np
import jax
import jax.numpy as jnp
from jax import lax
from jax.experimental import pallas as pl
from jax.experimental.pallas import tpu as pltpu

F32 = jnp.float32
BF16 = jnp.bfloat16

D_MODEL = 1024
D_HYENA = 256
RW_HEADS = 6
RW_HEAD_DIM = 64
D_RWKV = RW_HEADS * RW_HEAD_DIM
GDN_HEADS = 6
GDN_HEAD_DIM = 64
D_GDN = GDN_HEADS * GDN_HEAD_DIM
HY_BANDS = 16
HY_EMB = 1 + 2 * HY_BANDS
HY_FFN = 64
RW_LORA_W = 64
RW_LORA_A = 64
RW_LORA_G = 128
RW_DECAY_SCALE = 0.606531
RW_GN_EPS = 64e-5
XA_HEADS = 4
XA_HEAD_DIM = D_MODEL // XA_HEADS
D_FF = 2816
NORM_EPS = 1e-6
L2_EPS = 1e-6
HY_COLS = 3 * D_HYENA
RW_COLS = 3 * D_RWKV + 2 * RW_LORA_W + 2 * RW_LORA_A + RW_LORA_G
GDN_COLS = 4 * D_GDN + 4 * GDN_HEADS

LANES = 128
SUBLANES = 8
MXU_WIDTH = 256
VMEM_LIMIT_BYTES = 56 * 1024 * 1024
GDN_COLS_PAD = 4 * D_GDN + LANES
CHUNK = 64
CHUNKS_PER_STEP = 8
HEAD_DIM = 64
PAIR = 2 * HEAD_DIM
FFT_NB = LANES


def _cparams(*sem):
    return pltpu.CompilerParams(dimension_semantics=sem, vmem_limit_bytes=VMEM_LIMIT_BYTES)


def _full(shape):
    nd = len(shape)
    return pl.BlockSpec(shape, lambda *_: (0,) * nd)


NN = (((1,), (0,)), ((), ()))
NT = (((1,), (1,)), ((), ()))
TN = (((0,), (0,)), ((), ()))


def _dg(a, b, dims):
    return lax.dot_general(a, b, dims, preferred_element_type=F32)


def _split2(x):
    hi = x.astype(BF16)
    lo = (x - hi.astype(F32)).astype(BF16)
    return hi, lo


def _mm(a, b, dims=NN, passes=1):
    if passes == 1:
        return _dg(a.astype(BF16), b.astype(BF16), dims)
    ah, al = _split2(a)
    bh, bl = _split2(b)
    return _dg(ah, bh, dims) + (_dg(al, bh, dims) + _dg(ah, bl, dims))


def _mm_lconst(c, x):
    n = x.shape[1]
    y = _dg(c.astype(BF16), jnp.concatenate(_split2(x), axis=1), NN)
    return y[:, :n] + y[:, n:]


def _mm_rconst(x, c):
    m = x.shape[0]
    y = _dg(jnp.concatenate(_split2(x), axis=0), c.astype(BF16), NN)
    return y[:m] + y[m:]


def _mm_dft(m, x):
    return _dg(m.astype(BF16), x.astype(BF16), NN)


def _rms(x, g):
    return x * lax.rsqrt(jnp.mean(x * x, axis=-1, keepdims=True) + NORM_EPS) * g


def _sigmoid(x):
    return 1.0 / (1.0 + jnp.exp(-x))


def _silu(x):
    return x * _sigmoid(x)


def _softplus(x):
    return jnp.maximum(x, 0.0) + jnp.log(1.0 + jnp.exp(-jnp.abs(x)))


def _shift_rows(p, prev_row, next_row):
    n = p.shape[0]
    rows = lax.broadcasted_iota(jnp.int32, p.shape, 0)
    prev = jnp.where(rows == 0, prev_row, pltpu.roll(p, 1, 0))
    nxt = jnp.where(rows == n - 1, next_row, pltpu.roll(p, n - 1, 0))
    return prev, nxt


def _halo_specs(tm, width, L, col_block=0):
    r = tm // SUBLANES
    last = L // SUBLANES - 1

    def cur(b, i):
        return (b, i, col_block)

    def prev(b, i):
        return (b, jnp.maximum(i * r - 1, 0), col_block)

    def nxt(b, i):
        return (b, jnp.minimum((i + 1) * r, last), col_block)

    return (pl.BlockSpec((1, tm, width), cur),
            pl.BlockSpec((1, SUBLANES, width), prev),
            pl.BlockSpec((1, SUBLANES, width), nxt))


def _halo_rows(prev_ref, next_ref):
    i = pl.program_id(1)
    n = pl.num_programs(1)
    prev_row = jnp.where(i > 0, prev_ref[0, SUBLANES - 1:SUBLANES, :], 0.0)
    next_row = jnp.where(i < n - 1, next_ref[0, 0:1, :], 0.0)
    return prev_row, next_row


def _head_sum_matrix(width, head_dim, scale=1.0):
    idx = np.arange(width) // head_dim
    return jnp.asarray((idx[:, None] == idx[None, :]).astype(np.float32) * scale)


def _ffn_body(x_ref, g_ref, w1_ref, w3_ref, w2_ref, gf_ref, o_ref, acc_ref, *, n_chunks, tf, final):
    x = x_ref[...]
    h = _rms(x, g_ref[...]).astype(BF16)
    for j in range(n_chunks):
        sl = slice(j * tf, (j + 1) * tf)
        a = _dg(h, w1_ref[:, sl], NN)
        b = _dg(h, w3_ref[:, sl], NN)
        t = (_silu(a) * b).astype(BF16)
        part = _dg(t, w2_ref[sl, :], NN)
        if j == 0:
            acc_ref[...] = part
        else:
            acc_ref[...] += part
    y = x + 0.5 * acc_ref[...]
    if final:
        y = _rms(y, gf_ref[...])
    o_ref[...] = y


def _ffn(x, g, w1, w3, w2, gf, final):
    T, D = x.shape
    FF = w1.shape[1]
    tm = min(512, T)
    tf = MXU_WIDTH
    body = functools.partial(_ffn_body, n_chunks=FF // tf, tf=tf, final=final)
    return pl.pallas_call(
        body,
        out_shape=jax.ShapeDtypeStruct((T, D), F32),
        grid=(T // tm,),
        in_specs=[pl.BlockSpec((tm, D), lambda i: (i, 0)), _full((1, D)),
                  _full((D, FF)), _full((D, FF)), _full((FF, D)), _full((1, D))],
        out_specs=pl.BlockSpec((tm, D), lambda i: (i, 0)),
        scratch_shapes=[pltpu.VMEM((tm, D), F32)],
        compiler_params=_cparams("parallel"),
        name="ffn_final" if final else "ffn",
    )(x, g.reshape(1, D), w1, w3, w2, gf.reshape(1, D))


def _inproj_body(x_ref, g_ref, w_ref, ohy_ref, orw_ref, ogd_ref):
    h = _rms(x_ref[...], g_ref[...]).astype(BF16)
    c0, c1 = HY_COLS, HY_COLS + RW_COLS
    ohy_ref[...] = _dg(h, w_ref[:, 0:c0], NN)
    orw_ref[...] = _dg(h, w_ref[:, c0:c1], NN)
    ogd_ref[...] = _dg(h, w_ref[:, c1:c1 + GDN_COLS_PAD], NN)


def _inproj(x, g, w):
    T, D = x.shape
    tm = min(256, T)
    W = w.shape[1]
    widths = (HY_COLS, RW_COLS, GDN_COLS_PAD)
    return pl.pallas_call(
        _inproj_body,
        out_shape=tuple(jax.ShapeDtypeStruct((T, n), F32) for n in widths),
        grid=(T // tm,),
        in_specs=[pl.BlockSpec((tm, D), lambda i: (i, 0)), _full((1, D)), _full((D, W))],
        out_specs=tuple(pl.BlockSpec((tm, n), lambda i: (i, 0)) for n in widths),
        compiler_params=_cparams("parallel"),
        name="inproj",
    )(x, g.reshape(1, D), w)


def _hy_filter_body(z_ref, t_ref, freq_ref, w1_ref, b1_ref, w2_ref, b2_ref, w3_ref, dec_ref,
                    h_ref, s_ref):
    i = pl.program_id(0)
    freq = freq_ref[...]
    h = jnp.sin(freq * (_mm(z_ref[...], w1_ref[...], passes=3) + b1_ref[...]))
    h = jnp.sin(freq * (_mm(h, w2_ref[...], passes=3) + b2_ref[...]))
    h = _mm(h, w3_ref[...], passes=3) * jnp.exp(-t_ref[...] * dec_ref[...])
    C = D_HYENA
    fwd = h[:, :C]
    bwd = h[:, C:]
    rows = lax.broadcasted_iota(jnp.int32, bwd.shape, 0)
    bwd = jnp.where((rows == 0) & (i == 0), 0.0, bwd)
    h_ref[0] = fwd
    h_ref[1] = bwd
    part = jnp.sum(jnp.abs(fwd) + jnp.abs(bwd), axis=0, keepdims=True)

    @pl.when(i == 0)
    def _():
        s_ref[...] = part

    @pl.when(i > 0)
    def _():
        s_ref[...] += part


def _hy_filter(z, t, freq, w1, b1, w2, b2, w3, decay):
    L = z.shape[0]
    tl = min(1024, L)
    C = D_HYENA
    return pl.pallas_call(
        _hy_filter_body,
        out_shape=(jax.ShapeDtypeStruct((2, L, C), F32), jax.ShapeDtypeStruct((1, C), F32)),
        grid=(L // tl,),
        in_specs=[pl.BlockSpec((tl, LANES), lambda i: (i, 0)), pl.BlockSpec((tl, 1), lambda i: (i, 0)),
                  _full((1, HY_FFN)), _full((LANES, HY_FFN)), _full((1, HY_FFN)),
                  _full((HY_FFN, HY_FFN)), _full((1, HY_FFN)), _full((HY_FFN, 2 * C)), _full((1, 2 * C))],
        out_specs=(pl.BlockSpec((2, tl, C), lambda i: (0, i, 0)), _full((1, C))),
        compiler_params=_cparams("arbitrary"),
        name="hy_filter",
    )(z, t, freq.reshape(1, -1), w1, b1.reshape(1, -1), w2, b2.reshape(1, -1), w3, decay.reshape(1, -1))


def _hy_pre_body(p_ref, pp_ref, pn_ref, w_ref, b_ref, u_ref, x0_ref):
    p = p_ref[0]
    prev_row, next_row = _halo_rows(pp_ref, pn_ref)
    prev, nxt = _shift_rows(p, prev_row, next_row)
    c = w_ref[0:1, :] * prev + w_ref[1:2, :] * p + w_ref[2:3, :] * nxt + b_ref[...]
    C = D_HYENA
    x0_ref[0] = c[:, :C]
    u_ref[0] = c[:, C:2 * C] * c[:, 2 * C:]


def _hy_pre(p, w, b):
    B, L, W = p.shape
    tm = min(512, L)
    C = D_HYENA
    return pl.pallas_call(
        _hy_pre_body,
        out_shape=(jax.ShapeDtypeStruct((B, L, C), F32),) * 2,
        grid=(B, L // tm),
        in_specs=[*_halo_specs(tm, W, L), _full((3, W)), _full((1, W))],
        out_specs=(pl.BlockSpec((1, tm, C), lambda b, i: (b, i, 0)),) * 2,
        compiler_params=_cparams("parallel", "parallel"),
        name="hy_pre",
    )(p, p, p, w, b.reshape(1, W))


FFT_TJ = 16


def _fft_a_body(u_ref, m_ref, twr_ref, twi_ref, o_ref, acc_ref, *, na, tj):
    reps = u_ref.shape[-1] // LANES
    for j in range(tj):
        a = _mm_dft(m_ref[...], u_ref[0, :, j, :])
        ar, ai = a[:na], a[na:]
        twr = jnp.concatenate([twr_ref[j]] * reps, axis=1)
        twi = jnp.concatenate([twi_ref[j]] * reps, axis=1)
        acc_ref[0, :, j, :] = ar * twr - ai * twi
        acc_ref[1, :, j, :] = ar * twi + ai * twr
    o_ref[0] = acc_ref[...].astype(BF16)


def _fft_a(u4, m, twr, twi):
    B, half, nb, C = u4.shape
    na = 2 * half
    tj = FFT_TJ
    tw = pl.BlockSpec((tj, na, LANES), lambda b, j: (j, 0, 0))
    return pl.pallas_call(
        functools.partial(_fft_a_body, na=na, tj=tj),
        out_shape=jax.ShapeDtypeStruct((B, 2, na, nb, C), BF16),
        grid=(B, nb // tj),
        in_specs=[pl.BlockSpec((1, half, tj, C), lambda b, j: (b, 0, j, 0)), _full((2 * na, half)), tw, tw],
        out_specs=pl.BlockSpec((1, 2, na, tj, C), lambda b, j: (b, 0, 0, j, 0)),
        scratch_shapes=[pltpu.VMEM((2, na, tj, C), F32)],
        compiler_params=_cparams("parallel", "parallel"),
        name="fft_a",
    )(u4, m, twr, twi)


def _fft_filt_body(bf_ref, bb_ref, m_ref, s_ref, o_ref, *, tk):
    nb = FFT_NB
    inv = 1.0 / s_ref[...]
    for k in range(tk):
        xf = _mm_dft(m_ref[...], jnp.concatenate([bf_ref[0, 0, k], bf_ref[0, 1, k]], axis=0))
        xb = _mm_dft(m_ref[...], jnp.concatenate([bb_ref[0, 0, k], bb_ref[0, 1, k]], axis=0))
        o_ref[0, k] = (xf[:nb] + xb[:nb]) * inv
        o_ref[1, k] = (xf[nb:] - xb[nb:]) * inv


def _fft_filt(bt, m, s):
    _, _, na, nb, C = bt.shape
    tk = min(4, na)
    return pl.pallas_call(
        functools.partial(_fft_filt_body, tk=tk),
        out_shape=jax.ShapeDtypeStruct((2, na, nb, C), F32),
        grid=(na // tk,),
        in_specs=[pl.BlockSpec((1, 2, tk, nb, C), lambda i: (0, 0, i, 0, 0)),
                  pl.BlockSpec((1, 2, tk, nb, C), lambda i: (1, 0, i, 0, 0)),
                  _full((2 * nb, 2 * nb)), _full((1, C))],
        out_specs=pl.BlockSpec((2, tk, nb, C), lambda i: (0, i, 0, 0)),
        compiler_params=_cparams("parallel"),
        name="fft_filt",
    )(bt, bt, m, s)


def _fft_c_body(b_ref, k_ref, mf_ref, mi_ref, twr_ref, twi_ref, o_ref, *, tk):
    nb = FFT_NB
    for k in range(tk):
        x = _mm_dft(mf_ref[...], jnp.concatenate([b_ref[0, 0, k], b_ref[0, 1, k]], axis=0))
        xr, xi = x[:nb], x[nb:]
        kr, ki = k_ref[0, k], k_ref[1, k]
        y = jnp.concatenate([xr * kr - xi * ki, xr * ki + xi * kr], axis=0)
        d = _mm_dft(mi_ref[...], y)
        dr, di = d[:nb], d[nb:]
        reps = dr.shape[1] // LANES
        twr = jnp.concatenate([twr_ref[k]] * reps, axis=1)
        twi = jnp.concatenate([twi_ref[k]] * reps, axis=1)
        o_ref[0, 0, k] = (dr * twr + di * twi).astype(BF16)
        o_ref[0, 1, k] = (di * twr - dr * twi).astype(BF16)


def _fft_c(bt, khat, mf, mi, twr, twi):
    B, _, na, nb, C = bt.shape
    tk = min(4, na)
    blk = pl.BlockSpec((1, 2, tk, nb, C), lambda i, b: (b, 0, i, 0, 0))
    m = _full((2 * nb, 2 * nb))
    tw = pl.BlockSpec((tk, nb, LANES), lambda i, b: (i, 0, 0))
    return pl.pallas_call(
        functools.partial(_fft_c_body, tk=tk),
        out_shape=jax.ShapeDtypeStruct(bt.shape, BF16),
        grid=(na // tk, B),
        in_specs=[blk, pl.BlockSpec((2, tk, nb, C), lambda i, b: (0, i, 0, 0)), m, m, tw, tw],
        out_specs=blk,
        compiler_params=_cparams("parallel", "parallel"),
        name="fft_c",
    )(bt, khat, mf, mi, twr, twi)


def _fft_out_body(e_ref, m_ref, u_ref, x0_ref, bias_ref, o_ref, ef_ref, acc_ref, *, tj):
    ef_ref[...] = e_ref[0].astype(F32)
    for j in range(tj):
        e = jnp.concatenate([ef_ref[0, :, j, :], ef_ref[1, :, j, :]], axis=0)
        conv = _mm_dft(m_ref[...], e)
        acc_ref[:, j, :] = x0_ref[0, :, j, :] * (conv + bias_ref[...] * u_ref[0, :, j, :])
    o_ref[0] = acc_ref[...].astype(BF16)


def _fft_out(e, m, u4, x04, bias):
    B, half, nb, C = u4.shape
    na = 2 * half
    tj = FFT_TJ
    blk = pl.BlockSpec((1, half, tj, C), lambda b, j: (b, 0, j, 0))
    return pl.pallas_call(
        functools.partial(_fft_out_body, tj=tj),
        out_shape=jax.ShapeDtypeStruct(u4.shape, BF16),
        grid=(B, nb // tj),
        in_specs=[pl.BlockSpec((1, 2, na, tj, C), lambda b, j: (b, 0, 0, j, 0)), _full((half, 2 * na)), blk, blk,
                  _full((1, C))],
        out_specs=blk,
        scratch_shapes=[pltpu.VMEM((2, na, tj, C), F32), pltpu.VMEM((half, tj, C), F32)],
        compiler_params=_cparams("parallel", "parallel"),
        name="fft_out",
    )(e, m, u4, x04, bias)


@functools.lru_cache(maxsize=None)
def _fft_consts_np(L):
    n = 2 * L
    nb = FFT_NB
    na = n // nb
    half = na // 2
    ka = np.arange(na)
    fa = np.exp(-2j * np.pi * np.outer(ka, ka) / na)
    kb = np.arange(nb)
    fb = np.exp(-2j * np.pi * np.outer(kb, kb) / nb)
    m_a = np.concatenate([fa.real[:, :half], fa.imag[:, :half]], axis=0)
    m_f = np.block([[fb.real, -fb.imag], [fb.imag, fb.real]])
    m_i = np.block([[fb.real, fb.imag], [-fb.imag, fb.real]])
    m_o = np.concatenate([fa.real[:half, :], fa.imag[:half, :]], axis=1) / n
    tw = np.exp(-2j * np.pi * (np.outer(ka, kb) % n) / n)
    f32 = lambda a: np.asarray(a, np.float32)
    return f32(m_a), f32(m_f), f32(m_i), f32(m_o), f32(tw.real), f32(tw.imag)


def _hyena_pos(L):
    t = jnp.linspace(0.0, 1.0, L, dtype=F32)[:, None]
    ang = 2.0 * math.pi * jnp.arange(L, dtype=F32)[:, None] / L
    bands = jnp.linspace(1e-4, HY_BANDS - 1, HY_BANDS, dtype=F32)[None, :]
    z = jnp.concatenate([t, jnp.cos(bands * ang), -jnp.sin(bands * ang)], axis=-1)
    return jnp.pad(z, ((0, 0), (0, LANES - HY_EMB))), t


def _hyena(p_hy, z, t, prm, l):
    B, L, _ = p_hy.shape
    C = D_HYENA
    nb = FFT_NB
    na = 2 * L // nb
    half = na // 2
    m_a, m_f, m_i, m_o, twr, twi = _fft_consts_np(L)
    ma, mf, mi, mo = (jnp.asarray(m) for m in (m_a, m_f, m_i, m_o))
    twr_l = jnp.broadcast_to(jnp.asarray(twr)[:, :, None], (na, nb, LANES))
    twi_l = jnp.broadcast_to(jnp.asarray(twi)[:, :, None], (na, nb, LANES))
    twr_t = jnp.broadcast_to(jnp.asarray(twr.T)[:, :, None], (nb, na, LANES))
    twi_t = jnp.broadcast_to(jnp.asarray(twi.T)[:, :, None], (nb, na, LANES))

    w1 = jnp.pad(prm["hy_w1"][l], ((0, LANES - HY_EMB), (0, 0)))
    hraw, hsum = _hy_filter(z, t, prm["hy_freq"][l], w1, prm["hy_b1"][l], prm["hy_w2"][l],
                            prm["hy_b2"][l], prm["hy_w3"][l], prm["hy_decay"][l])
    khat = _fft_filt(_fft_a(hraw.reshape(2, half, nb, C), ma, twr_t, twi_t), mf, hsum)

    u, x0 = _hy_pre(p_hy, prm["hy_conv_w"][l], prm["hy_conv_b"][l])
    u4 = u.reshape(B, half, nb, C)
    e = _fft_c(_fft_a(u4, ma, twr_t, twi_t), khat, mf, mi, twr_l, twi_l)
    y = _fft_out(e, mo, u4, x0.reshape(B, half, nb, C), prm["hy_bias"][l].reshape(1, C))
    return y.reshape(B * L, C)


def _head_block_mask():
    r = lax.broadcasted_iota(jnp.int32, (PAIR, PAIR), 0)
    c = lax.broadcasted_iota(jnp.int32, (PAIR, PAIR), 1)
    return ((r // HEAD_DIM) == (c // HEAD_DIM)).astype(F32)


def _bd(x, hm):
    return jnp.concatenate([x, x], axis=0) * hm


def _cat_masks(reverse):
    r = lax.broadcasted_iota(jnp.int32, (CHUNK, PAIR), 0)
    s = lax.broadcasted_iota(jnp.int32, (CHUNK, PAIR), 1) % CHUNK
    if reverse:
        return s > r, s >= r, s == r
    return s < r, s <= r, s == r


NEUMANN_PASSES = 1
GRAM_PASSES = 1


def _neumann_inverse(a, diag, hm):
    C = CHUNK
    t = jnp.where(diag, 1.0, 0.0) + a
    p = _mm(a, _bd(a, hm), passes=NEUMANN_PASSES)
    yield
    steps = int(math.log2(C)) - 1
    for _ in range(steps - 1):
        pt = _mm(jnp.concatenate([p, t], axis=0), _bd(p, hm), passes=NEUMANN_PASSES)
        yield
        p, t = pt[:C], t + pt[C:]
    return t + _mm(t, _bd(p, hm), passes=NEUMANN_PASSES)


def _interleave(gens):
    results = [None] * len(gens)
    alive = list(range(len(gens)))
    while alive:
        for i in list(alive):
            try:
                next(gens[i])
            except StopIteration as done:
                results[i] = done.value
                alive.remove(i)
    return results


def _rw_prep_body(p_ref, pp_ref, pn_ref, mup_ref, mun_ref, wl_ref, w0_ref, al_ref, a0_ref, gl_ref,
                  kk_ref, ka_ref, rk_ref, hs_ref,
                  v_o, gate_o, bonus_o, atf_o, rtf_o, khf_o, bhf_o, atb_o, rtb_o, khb_o, bhb_o, ptf_o, ptb_o):
    p = p_ref[0]
    tm = p.shape[0]
    prev_row, next_row = _halo_rows(pp_ref, pn_ref)
    prev, nxt = _shift_rows(p, prev_row, next_row)
    p = p + mup_ref[...] * (prev - p) + mun_ref[...] * (nxt - p)
    D = D_RWKV
    r, k, v = p[:, 0:D], p[:, D:2 * D], p[:, 2 * D:3 * D]
    o = 3 * D
    lw = (p[:, o:o + RW_LORA_W], p[:, o + RW_LORA_W:o + 2 * RW_LORA_W])
    o += 2 * RW_LORA_W
    la = (p[:, o:o + RW_LORA_A], p[:, o + RW_LORA_A:o + 2 * RW_LORA_A])
    o += 2 * RW_LORA_A
    lg = p[:, o:o + RW_LORA_G]
    hs = hs_ref[...]
    gate_o[0] = _mm(_sigmoid(lg), gl_ref[...], passes=3)
    k2 = k * kk_ref[...]
    kkn = k2 * lax.rsqrt(_mm_rconst(k2 * k2, hs) + L2_EPS)
    v_o[0] = v.astype(BF16)
    bonus_o[0] = _mm_rconst(r * k * rk_ref[...], hs) * v
    outs = ((atf_o, rtf_o, khf_o, bhf_o, ptf_o), (atb_o, rtb_o, khb_o, bhb_o, ptb_o))
    tot_sel = _chunk_total_matrix(tm)
    for d in range(2):
        at_o, rt_o, kh_o, bh_o, pt_o = outs[d]
        logw = -RW_DECAY_SCALE * _sigmoid(w0_ref[d:d + 1, :] + _mm(jnp.tanh(lw[d]), wl_ref[d], passes=3))
        a = _sigmoid(a0_ref[d:d + 1, :] + _mm(la[d], al_ref[d], passes=3))
        kd = k * (1.0 + (a - 1.0) * ka_ref[...])
        cum = _mm_lconst(_chunk_cum_matrix(tm, bool(d)), logw)
        pinv = jnp.exp(-cum)
        at_o[0] = (-kkn * jnp.exp(cum - logw)).astype(BF16)
        rt_o[0] = (r * jnp.exp(cum)).astype(BF16)
        kh_o[0] = (kd * pinv).astype(BF16)
        bh_o[0] = (kkn * a * pinv).astype(BF16)
        pt_o[0] = jnp.exp(_mm_lconst(tot_sel, logw))


def _chunk_cum_matrix(tm, reverse):
    r = lax.broadcasted_iota(jnp.int32, (tm, tm), 0)
    c = lax.broadcasted_iota(jnp.int32, (tm, tm), 1)
    same = (r // CHUNK) == (c // CHUNK)
    return (same & ((c >= r) if reverse else (c <= r))).astype(F32)


def _chunk_total_matrix(tm):
    r = lax.broadcasted_iota(jnp.int32, (tm // SUBLANES, tm), 0)
    c = lax.broadcasted_iota(jnp.int32, (tm // SUBLANES, tm), 1)
    return (r // (CHUNK // SUBLANES) == c // CHUNK).astype(F32)


def _rw_prep(p, prm, l):
    B, L, W = p.shape
    tm = min(256, L)
    D = D_RWKV
    row = lambda a: a.reshape(1, -1)
    out = pl.BlockSpec((1, tm, D), lambda b, i: (b, i, 0))
    tot = pl.BlockSpec((1, tm // SUBLANES, D), lambda b, i: (b, i, 0))
    full = jax.ShapeDtypeStruct((B, L, D), F32)
    half = jax.ShapeDtypeStruct((B, L, D), BF16)
    small = jax.ShapeDtypeStruct((B, L // SUBLANES, D), F32)
    return pl.pallas_call(
        _rw_prep_body,
        out_shape=(half, full, full) + (half,) * 8 + (small,) * 2,
        grid=(B, L // tm),
        in_specs=[*_halo_specs(tm, W, L), _full((1, W)), _full((1, W)),
                  _full((2, RW_LORA_W, D)), _full((2, D)), _full((2, RW_LORA_A, D)), _full((2, D)),
                  _full((RW_LORA_G, D)), _full((1, D)), _full((1, D)), _full((1, D)), _full((D, D))],
        out_specs=(out,) * 11 + (tot,) * 2,
        compiler_params=_cparams("parallel", "parallel"),
        name="rw_prep",
    )(p, p, p, row(prm["rw_mu_prev"][l]), row(prm["rw_mu_next"][l]), prm["rw_w_lora"][l], prm["rw_w0"][l],
      prm["rw_a_lora"][l], prm["rw_a0"][l], prm["rw_g_lora"][l], row(prm["rw_k_k"][l]),
      row(prm["rw_k_a"][l]), row(prm["rw_r_k"][l]), _head_sum_matrix(D, RW_HEAD_DIM))


def _rw_chunk(v, at, rt, kh, bh, ptot, s_in, s_out, reverse, hm):
    C = CHUNK
    strict, incl, diag = _cat_masks(reverse)
    kb_bd = jnp.concatenate([_bd(kh, hm), _bd(bh, hm)], axis=0)
    g = _mm(jnp.concatenate([at, rt], axis=0), kb_bd, NT, passes=GRAM_PASSES)
    yield
    n = 2 * C
    a_ak = jnp.where(strict, g[:C, :n], 0.0)
    a_ab = jnp.where(strict, g[:C, n:], 0.0)
    a_rk = jnp.where(incl, g[C:, :n], 0.0)
    a_rb = jnp.where(incl, g[C:, n:], 0.0)
    av = _mm(jnp.concatenate([a_ak, a_rk], axis=0), _bd(v, hm))
    t = yield from _neumann_inverse(a_ab, diag, hm)
    yield
    uw = _mm(t, jnp.concatenate([_bd(av[:C], hm), _bd(at, hm)], axis=1))
    yield
    while s_in[0] is None:
        yield
    s0 = s_in[0]
    ws = _mm(jnp.concatenate([uw[:, PAIR:], rt], axis=0), s0, NT)
    yield
    u = uw[:, :PAIR] + ws[:C]
    upd = _mm(jnp.concatenate([v, u], axis=0), jnp.concatenate([kh, bh], axis=0), TN) * hm
    s_out[0] = (s0 + upd) * ptot
    return ws[C:] + av[C:] + _mm(a_rb, _bd(u, hm))


def _scan_schedule(n_pairs, make_chain, s_ref):
    gens, outs, finals = [], [], []
    for c in range(CHUNKS_PER_STEP):
        for d in range(2):
            ci = CHUNKS_PER_STEP - 1 - c if d else c
            rows = slice(ci * CHUNK, (ci + 1) * CHUNK)
            for j in range(n_pairs):
                if c == 0:
                    finals.append([s_ref[d, j]])
                k = d * n_pairs + j
                s_in = finals[k]
                s_out = [None]
                finals[k] = s_out
                gens.append(make_chain(d, j, rows, s_in, s_out))
                outs.append((d, j, rows))
    return gens, outs, finals


def _rw_scan_body(vf, atf, rtf, khf, bhf, ptf, vb, atb, rtb, khb, bhb, ptb, yf_o, yb_o, s_ref):
    @pl.when(pl.program_id(1) == 0)
    def _():
        s_ref[...] = jnp.zeros(s_ref.shape, F32)

    hm = _head_block_mask()
    dirs = ((vf, atf, rtf, khf, bhf, ptf, yf_o), (vb, atb, rtb, khb, bhb, ptb, yb_o))
    n_pairs = D_RWKV // PAIR

    def make_chain(d, j, rows, s_in, s_out):
        sl = slice(j * PAIR, (j + 1) * PAIR)
        v, at, rt, kh, bh, pt, _ = dirs[d]
        t0 = rows.start // SUBLANES
        return _rw_chunk(v[0, rows, sl], at[0, rows, sl], rt[0, rows, sl], kh[0, rows, sl], bh[0, rows, sl],
                         pt[0, t0:t0 + 1, sl], s_in, s_out, bool(d), hm)

    gens, outs, finals = _scan_schedule(n_pairs, make_chain, s_ref)
    for y, (d, j, rows) in zip(_interleave(gens), outs):
        dirs[d][-1][0, rows, j * PAIR:(j + 1) * PAIR] = y
    for k, cell in enumerate(finals):
        s_ref[k // n_pairs, k % n_pairs] = cell[0]


def _rw_scan(v, atf, rtf, khf, bhf, ptf, atb, rtb, khb, bhb, ptb):
    B, L, D = v.shape
    blk = CHUNKS_PER_STEP * CHUNK
    n = L // blk
    fw = pl.BlockSpec((1, blk, D), lambda b, i: (b, i, 0))
    bw = pl.BlockSpec((1, blk, D), lambda b, i: (b, n - 1 - i, 0))
    fwt = pl.BlockSpec((1, blk // SUBLANES, D), lambda b, i: (b, i, 0))
    bwt = pl.BlockSpec((1, blk // SUBLANES, D), lambda b, i: (b, n - 1 - i, 0))
    return pl.pallas_call(
        _rw_scan_body,
        out_shape=(jax.ShapeDtypeStruct((B, L, D), F32),) * 2,
        grid=(B, n),
        in_specs=[fw] * 5 + [fwt] + [bw] * 5 + [bwt],
        out_specs=(fw, bw),
        scratch_shapes=[pltpu.VMEM((2, D // PAIR, PAIR, PAIR), F32)],
        compiler_params=_cparams("parallel", "arbitrary"),
        name="rw_scan",
    )(v, atf, rtf, khf, bhf, ptf, v, atb, rtb, khb, bhb, ptb)


GDN_G_LANE = 0
GDN_BETA_LANE = 2 * GDN_HEADS


def _gdn_prep_body(p_ref, pp_ref, pn_ref, s_ref, w_ref, alog_ref, dt_ref, hs_ref,
                   q_o, k_o, v_o, gcf_o, gcb_o, bf_o, bb_o):
    p = p_ref[0]
    tm = p.shape[0]
    prev_row, next_row = _halo_rows(pp_ref, pn_ref)
    prev, nxt = _shift_rows(p, prev_row, next_row)
    c = _silu(w_ref[0:1, :] * prev + w_ref[1:2, :] * p + w_ref[2:3, :] * nxt)
    D = D_GDN
    q, k, v = c[:, 0:D], c[:, D:2 * D], c[:, 2 * D:3 * D]
    hs = hs_ref[...]
    q_o[0] = q * lax.rsqrt(_mm_rconst(q * q, hs) + L2_EPS) * (GDN_HEAD_DIM ** -0.5)
    k_o[0] = k * lax.rsqrt(_mm_rconst(k * k, hs) + L2_EPS)
    v_o[0] = v
    s = s_ref[0]
    lane = lax.broadcasted_iota(jnp.int32, s.shape, 1)
    g = -jnp.exp(alog_ref[...]) * _softplus(s + dt_ref[...])
    gs = jnp.where(lane < GDN_BETA_LANE, g, jnp.where(lane < 4 * GDN_HEADS, _sigmoid(s), 0.0))
    er = lax.broadcasted_iota(jnp.int32, (LANES, 4 * D), 0)
    ec = lax.broadcasted_iota(jnp.int32, (LANES, 4 * D), 1)
    gx = _mm_rconst(gs, (er == (ec // D) * GDN_HEADS + (ec % D) // GDN_HEAD_DIM).astype(F32))
    gcf_o[0] = _mm_lconst(_chunk_cum_matrix(tm, False), gx[:, 0:D])
    gcb_o[0] = _mm_lconst(_chunk_cum_matrix(tm, True), gx[:, D:2 * D])
    bf_o[0] = gx[:, 2 * D:3 * D]
    bb_o[0] = gx[:, 3 * D:]


def _gdn_prep(p, prm, l):
    B, L, W = p.shape
    tm = min(256, L)
    D = D_GDN
    Wq = 3 * D
    pad = LANES - 2 * GDN_HEADS
    alog = jnp.pad(prm["gdn_a_log"][l].reshape(-1), (0, pad)).reshape(1, LANES)
    dt = jnp.pad(prm["gdn_dt_bias"][l].reshape(-1), (0, pad)).reshape(1, LANES)
    out = pl.BlockSpec((1, tm, D), lambda b, i: (b, i, 0))
    side = pl.BlockSpec((1, tm, LANES), lambda b, i: (b, i, 4 * D // LANES))
    return pl.pallas_call(
        _gdn_prep_body,
        out_shape=(jax.ShapeDtypeStruct((B, L, D), F32),) * 7,
        grid=(B, L // tm),
        in_specs=[*_halo_specs(tm, Wq, L), side, _full((3, Wq)), _full((1, LANES)), _full((1, LANES)),
                  _full((D, D))],
        out_specs=(out,) * 7,
        compiler_params=_cparams("parallel", "parallel"),
        name="gdn_prep",
    )(p, p, p, p, prm["gdn_conv_w"][l], alog, dt, _head_sum_matrix(D, GDN_HEAD_DIM))


def _gdn_chunk(q, k, v, gcx, beta, s_in, s_out, reverse, hm):
    C = CHUNK
    strict, incl, diag = _cat_masks(reverse)
    grow = _mm_lconst(jnp.ones((C, C), F32), jnp.where(diag, gcx, 0.0))
    kb = k * beta
    kq = _mm(jnp.concatenate([kb, q], axis=0), _bd(k, hm), NT, passes=GRAM_PASSES)
    yield
    dec = jnp.exp(jnp.where(incl, gcx - grow, -1e30))
    a = jnp.where(strict, kq[:C] * dec, 0.0)
    attn = kq[C:] * dec
    t = yield from _neumann_inverse(-a, diag, hm)
    yield
    egc = jnp.exp(gcx)
    sol = _mm(t, jnp.concatenate([_bd(v * beta, hm), _bd(kb * egc, hm)], axis=1))
    yield
    while s_in[0] is None:
        yield
    s0 = s_in[0]
    ws = _mm(jnp.concatenate([sol[:, PAIR:], q * egc], axis=0), s0)
    yield
    v_new = sol[:, :PAIR] - ws[:C]
    glast = gcx[0:1] if reverse else gcx[C - 1:C]
    upd = _mm(k * jnp.exp(glast - gcx), v_new, TN) * hm
    s_out[0] = s0 * jnp.exp(glast) + upd
    return ws[C:] + _mm(attn, _bd(v_new, hm))


def _gdn_scan_body(qf, kf, vf, gf, btf, qb, kb, vb, gb, btb, of_o, ob_o, s_ref):
    @pl.when(pl.program_id(1) == 0)
    def _():
        s_ref[...] = jnp.zeros(s_ref.shape, F32)

    hm = _head_block_mask()
    dirs = ((qf, kf, vf, gf, btf, of_o), (qb, kb, vb, gb, btb, ob_o))
    n_pairs = D_GDN // PAIR

    def make_chain(d, j, rows, s_in, s_out):
        sl = slice(j * PAIR, (j + 1) * PAIR)
        q, k, v, g, bt, _ = dirs[d]
        return _gdn_chunk(q[0, rows, sl], k[0, rows, sl], v[0, rows, sl], g[0, rows, sl], bt[0, rows, sl],
                          s_in, s_out, bool(d), hm)

    gens, outs, finals = _scan_schedule(n_pairs, make_chain, s_ref)
    for o, (d, j, rows) in zip(_interleave(gens), outs):
        dirs[d][-1][0, rows, j * PAIR:(j + 1) * PAIR] = o
    for k, cell in enumerate(finals):
        s_ref[k // n_pairs, k % n_pairs] = cell[0]


def _gdn_scan(q, k, v, gcf, gcb, btf, btb):
    B, L, D = q.shape
    blk = CHUNKS_PER_STEP * CHUNK
    n = L // blk
    fw = pl.BlockSpec((1, blk, D), lambda b, i: (b, i, 0))
    bw = pl.BlockSpec((1, blk, D), lambda b, i: (b, n - 1 - i, 0))
    return pl.pallas_call(
        _gdn_scan_body,
        out_shape=(jax.ShapeDtypeStruct((B, L, D), F32),) * 2,
        grid=(B, n),
        in_specs=[fw] * 5 + [bw] * 5,
        out_specs=(fw, bw),
        scratch_shapes=[pltpu.VMEM((2, D // PAIR, PAIR, PAIR), F32)],
        compiler_params=_cparams("parallel", "arbitrary"),
        name="gdn_scan",
    )(q, k, v, gcf, btf, q, k, v, gcb, btb)


def _mix_out_body(x_ref, yhy_ref, yf_ref, yb_ref, gate_ref, bonus_ref, of_ref, ob_ref, zg_ref,
                  gnw_ref, gnb_ref, nw_ref, avg_ref, w_ref, o_ref):
    avg = avg_ref[...]
    y = yf_ref[...] + yb_ref[...]
    mu = _mm_rconst(y, avg)
    dlt = y - mu
    var = _mm_rconst(dlt * dlt, avg)
    y_rw = (dlt * lax.rsqrt(var + RW_GN_EPS) * gnw_ref[...] + gnb_ref[...] + bonus_ref[...]) * gate_ref[...]
    o = of_ref[...] + ob_ref[...]
    ms = _mm_rconst(o * o, avg)
    y_gdn = o * lax.rsqrt(ms + NORM_EPS) * nw_ref[...] * _silu(zg_ref[...])
    c0, c1 = D_HYENA, D_HYENA + D_RWKV
    acc = _dg(yhy_ref[...].astype(BF16), w_ref[0:c0, :], NN)
    acc += _dg(y_rw.astype(BF16), w_ref[c0:c1, :], NN)
    acc += _dg(y_gdn.astype(BF16), w_ref[c1:, :], NN)
    o_ref[...] = x_ref[...] + acc


def _mix_out(x, y_hy, yf, yb, gate, bonus, of, ob, p_gdn, prm, l, w_out):
    T, D = x.shape
    tm = min(512, T)
    Dh = D_RWKV
    row = lambda a: a.reshape(1, -1)
    tile = lambda n: pl.BlockSpec((tm, n), lambda i: (i, 0))
    zg = pl.BlockSpec((tm, D_GDN), lambda i: (i, 3))
    avg = _head_sum_matrix(Dh, HEAD_DIM, 1.0 / HEAD_DIM)
    return pl.pallas_call(
        _mix_out_body,
        out_shape=jax.ShapeDtypeStruct((T, D), F32),
        grid=(T // tm,),
        in_specs=[tile(D), tile(D_HYENA)] + [tile(Dh)] * 6 + [zg, _full((1, Dh)), _full((1, Dh)),
                  _full((1, Dh)), _full((Dh, Dh)), _full((D, D))],
        out_specs=tile(D),
        compiler_params=_cparams("parallel"),
        name="mix_out",
    )(x, y_hy, yf, yb, gate, bonus, of, ob, p_gdn, row(prm["rw_gn_w"][l]), row(prm["rw_gn_b"][l]),
      row(jnp.tile(prm["gdn_norm_w"][l], GDN_HEADS)), avg, w_out)


def _mem_kv_body(m_ref, g_ref, wk_ref, wv_ref, k_o, v_o):
    h = _rms(m_ref[0], g_ref[...]).astype(BF16)
    k_o[0] = _dg(h, wk_ref[...], NN).astype(BF16)
    v_o[0] = _dg(h, wv_ref[...], NN).astype(BF16)


def _mem_kv(mem, g, wk, wv):
    B, M, D = mem.shape
    blk = pl.BlockSpec((1, M, D), lambda b: (b, 0, 0))
    return pl.pallas_call(
        _mem_kv_body,
        out_shape=(jax.ShapeDtypeStruct((B, M, D), BF16),) * 2,
        grid=(B,),
        in_specs=[blk, _full((1, D)), _full((D, D)), _full((D, D))],
        out_specs=(blk, blk),
        compiler_params=_cparams("parallel"),
        name="mem_kv",
    )(mem, g.reshape(1, D), wk, wv)


def _xattn_body(x_ref, g_ref, wq_ref, k_ref, v_ref, wo_ref, o_ref):
    x = x_ref[0]
    h = _rms(x, g_ref[...]).astype(BF16)
    q = (_dg(h, wq_ref[...], NN) * (XA_HEAD_DIM ** -0.5)).astype(BF16)
    outs = []
    for hd in range(XA_HEADS):
        sl = slice(hd * XA_HEAD_DIM, (hd + 1) * XA_HEAD_DIM)
        s = _dg(q[:, sl], k_ref[0, :, sl], NT)
        s = s - jnp.max(s, axis=-1, keepdims=True)
        e = jnp.exp(s)
        pr = e / jnp.sum(e, axis=-1, keepdims=True)
        outs.append(_dg(pr.astype(BF16), v_ref[0, :, sl], NN))
    o = jnp.concatenate(outs, axis=1).astype(BF16)
    o_ref[0] = x + _dg(o, wo_ref[...], NN)


def _xattn(x, g, wq, k, v, wo):
    B, L, D = x.shape
    M = k.shape[1]
    tm = min(512, L)
    tile = pl.BlockSpec((1, tm, D), lambda b, i: (b, i, 0))
    kv = pl.BlockSpec((1, M, D), lambda b, i: (b, 0, 0))
    return pl.pallas_call(
        _xattn_body,
        out_shape=jax.ShapeDtypeStruct((B, L, D), F32),
        grid=(B, L // tm),
        in_specs=[tile, _full((1, D)), _full((D, D)), kv, kv, _full((D, D))],
        out_specs=tile,
        compiler_params=_cparams("parallel", "parallel"),
        name="xattn",
    )(x, g.reshape(1, D), wq, k, v, wo)


def kernel(x, mem, norm_ffn1, ffn1_w1, ffn1_w3, ffn1_w2, norm_mix, w_in, w_out, hy_conv_w, hy_conv_b, hy_freq, hy_w1, hy_b1, hy_w2, hy_b2, hy_w3, hy_decay, hy_bias, rw_mu_prev, rw_mu_next, rw_w_lora, rw_w0, rw_a_lora, rw_a0, rw_g_lora, rw_k_k, rw_k_a, rw_r_k, rw_gn_w, rw_gn_b, gdn_conv_w, gdn_a_log, gdn_dt_bias, gdn_norm_w, norm_xattn, xa_wq, xa_wk, xa_wv, xa_wo, mem_norm, norm_ffn2, ffn2_w1, ffn2_w3, ffn2_w2, norm_final):
    prm = dict(hy_conv_w=hy_conv_w, hy_conv_b=hy_conv_b, hy_freq=hy_freq, hy_w1=hy_w1, hy_b1=hy_b1,
               hy_w2=hy_w2, hy_b2=hy_b2, hy_w3=hy_w3, hy_decay=hy_decay, hy_bias=hy_bias,
               rw_mu_prev=rw_mu_prev, rw_mu_next=rw_mu_next, rw_w_lora=rw_w_lora, rw_w0=rw_w0,
               rw_a_lora=rw_a_lora, rw_a0=rw_a0, rw_g_lora=rw_g_lora, rw_k_k=rw_k_k, rw_k_a=rw_k_a,
               rw_r_k=rw_r_k, rw_gn_w=rw_gn_w, rw_gn_b=rw_gn_b, gdn_conv_w=gdn_conv_w,
               gdn_a_log=gdn_a_log, gdn_dt_bias=gdn_dt_bias, gdn_norm_w=gdn_norm_w)
    B, L, D = x.shape
    depth = norm_ffn1.shape[0]
    T = B * L
    z_pos, t_pos = _hyena_pos(L)
    bf = lambda w: w.astype(BF16)
    w_in_p = jnp.pad(w_in, ((0, 0), (0, 0), (0, GDN_COLS_PAD - GDN_COLS)))
    xt = x.reshape(T, D)
    for l in range(depth):
        xt = _ffn(xt, norm_ffn1[l], bf(ffn1_w1[l]), bf(ffn1_w3[l]), bf(ffn1_w2[l]), norm_final, False)
        p_hy, p_rw, p_gdn = _inproj(xt, norm_mix[l], bf(w_in_p[l]))
        y_hy = _hyena(p_hy.reshape(B, L, -1), z_pos, t_pos, prm, l)
        v, gate, bonus, *rw_ops = _rw_prep(p_rw.reshape(B, L, -1), prm, l)
        yf, yb = _rw_scan(v, *rw_ops[0:4], rw_ops[8], *rw_ops[4:8], rw_ops[9])
        of, ob = _gdn_scan(*_gdn_prep(p_gdn.reshape(B, L, -1), prm, l))
        flat = lambda a: a.reshape(T, -1)
        xt = _mix_out(xt, y_hy, flat(yf), flat(yb), flat(gate), flat(bonus), flat(of), flat(ob), p_gdn,
                      prm, l, bf(w_out[l]))
        km, vm = _mem_kv(mem, mem_norm, bf(xa_wk[l]), bf(xa_wv[l]))
        xt = _xattn(xt.reshape(B, L, D), norm_xattn[l], bf(xa_wq[l]), km, vm, bf(xa_wo[l])).reshape(T, D)
        xt = _ffn(xt, norm_ffn2[l], bf(ffn2_w1[l]), bf(ffn2_w3[l]), bf(ffn2_w2[l]), norm_final,
                  l == depth - 1)
    return xt.reshape(B, L, D)
```

```python
import functools
import math

import numpy as np
import jax
import jax.numpy as jnp
from jax import lax
from jax.experimental import pallas as pl
from jax.experimental.pallas import tpu as pltpu

F32 = jnp.float32
BF16 = jnp.bfloat16

D_MODEL = 1024
D_HYENA = 256
RW_HEADS = 6
RW_HEAD_DIM = 64
D_RWKV = RW_HEADS * RW_HEAD_DIM
GDN_HEADS = 6
GDN_HEAD_DIM = 64
D_GDN = GDN_HEADS * GDN_HEAD_DIM
HY_BANDS = 16
HY_EMB = 1 + 2 * HY_BANDS
HY_FFN = 64
RW_LORA_W = 64
RW_LORA_A = 64
RW_LORA_G = 128
RW_DECAY_SCALE = 0.606531
RW_GN_EPS = 64e-5
XA_HEADS = 4
XA_HEAD_DIM = D_MODEL // XA_HEADS
D_FF = 2816
NORM_EPS = 1e-6
L2_EPS = 1e-6
HY_COLS = 3 * D_HYENA
RW_COLS = 3 * D_RWKV + 2 * RW_LORA_W + 2 * RW_LORA_A + RW_LORA_G
GDN_COLS = 4 * D_GDN + 4 * GDN_HEADS

LANES = 128
SUBLANES = 8
MXU_WIDTH = 256
VMEM_LIMIT_BYTES = 56 * 1024 * 1024
GDN_COLS_PAD = 4 * D_GDN + LANES
CHUNK = 64
CHUNKS_PER_STEP = 8
HEAD_DIM = 64
PAIR = 2 * HEAD_DIM
FFT_NB = LANES


def _cparams(*sem):
    return pltpu.CompilerParams(dimension_semantics=sem, vmem_limit_bytes=VMEM_LIMIT_BYTES)


def _full(shape, single=False):
    nd = len(shape)
    if single:
        return pl.BlockSpec(shape, lambda *_: (0,) * nd, pipeline_mode=pl.Buffered(1))
    return pl.BlockSpec(shape, lambda *_: (0,) * nd)


NN = (((1,), (0,)), ((), ()))
NT = (((1,), (1,)), ((), ()))
TN = (((0,), (0,)), ((), ()))


def _dg(a, b, dims):
    return lax.dot_general(a, b, dims, preferred_element_type=F32)


def _split2(x):
    hi = x.astype(BF16)
    lo = (x - hi.astype(F32)).astype(BF16)
    return hi, lo


def _mm(a, b, dims=NN, passes=1):
    if passes == 1:
        return _dg(a.astype(BF16), b.astype(BF16), dims)
    ah, al = _split2(a)
    bh, bl = _split2(b)
    return _dg(ah, bh, dims) + (_dg(al, bh, dims) + _dg(ah, bl, dims))


def _mm_lconst(c, x):
    n = x.shape[1]
    y = _dg(c.astype(BF16), jnp.concatenate(_split2(x), axis=1), NN)
    return y[:, :n] + y[:, n:]


def _mm_rconst(x, c):
    m = x.shape[0]
    y = _dg(jnp.concatenate(_split2(x), axis=0), c.astype(BF16), NN)
    return y[:m] + y[m:]


def _head_stat(x, c):
    return _dg(x.astype(BF16), c.astype(BF16), NN)


def _mm_dft(m, x):
    return _dg(m.astype(BF16), x.astype(BF16), NN)


def _rms(x, g):
    return x * lax.rsqrt(jnp.mean(x * x, axis=-1, keepdims=True) + NORM_EPS) * g


def _sigmoid(x):
    return 0.5 * jnp.tanh(0.5 * x) + 0.5


def _silu(x):
    return x * _sigmoid(x)


def _softplus(x):
    return jnp.maximum(x, 0.0) + jnp.log(1.0 + jnp.exp(-jnp.abs(x)))


def _shift_rows(p, prev_row, next_row):
    n = p.shape[0]
    rows = lax.broadcasted_iota(jnp.int32, p.shape, 0)
    prev = jnp.where(rows == 0, prev_row, pltpu.roll(p, 1, 0))
    nxt = jnp.where(rows == n - 1, next_row, pltpu.roll(p, n - 1, 0))
    return prev, nxt


def _halo_specs(tm, width, L, col_block=0):
    r = tm // SUBLANES
    last = L // SUBLANES - 1

    def cur(b, i):
        return (b, i, col_block)

    def prev(b, i):
        return (b, jnp.maximum(i * r - 1, 0), col_block)

    def nxt(b, i):
        return (b, jnp.minimum((i + 1) * r, last), col_block)

    return (pl.BlockSpec((1, tm, width), cur),
            pl.BlockSpec((1, SUBLANES, width), prev),
            pl.BlockSpec((1, SUBLANES, width), nxt))


def _halo_rows(prev_ref, next_ref):
    i = pl.program_id(1)
    n = pl.num_programs(1)
    prev_row = jnp.where(i > 0, prev_ref[0, SUBLANES - 1:SUBLANES, :], 0.0)
    next_row = jnp.where(i < n - 1, next_ref[0, 0:1, :], 0.0)
    return prev_row, next_row


def _head_sum_matrix(width, head_dim, scale=1.0):
    idx = np.arange(width) // head_dim
    return jnp.asarray((idx[:, None] == idx[None, :]).astype(np.float32) * scale)


def _ffn_body(x_ref, g_ref, w1_ref, w3_ref, w2_ref, gf_ref, o_ref, acc_ref, *, n_chunks, tf, final):
    x = x_ref[...]
    h = _rms(x, g_ref[...]).astype(BF16)
    for j in range(n_chunks):
        sl = slice(j * tf, (j + 1) * tf)
        a = _dg(h, w1_ref[:, sl], NN)
        b = _dg(h, w3_ref[:, sl], NN)
        t = (_silu(a) * b).astype(BF16)
        part = _dg(t, w2_ref[sl, :], NN)
        if j == 0:
            acc_ref[...] = part
        else:
            acc_ref[...] += part
    y = x + 0.5 * acc_ref[...]
    if final:
        y = _rms(y, gf_ref[...])
    o_ref[...] = y


def _ffn(x, g, w1, w3, w2, gf, final):
    T, D = x.shape
    FF = w1.shape[1]
    tm = min(1024, T)
    tf = MXU_WIDTH
    body = functools.partial(_ffn_body, n_chunks=FF // tf, tf=tf, final=final)
    return pl.pallas_call(
        body,
        out_shape=jax.ShapeDtypeStruct((T, D), F32),
        grid=(T // tm,),
        in_specs=[pl.BlockSpec((tm, D), lambda i: (i, 0)), _full((1, D)),
                  _full((D, FF), True), _full((D, FF), True), _full((FF, D), True), _full((1, D))],
        out_specs=pl.BlockSpec((tm, D), lambda i: (i, 0)),
        scratch_shapes=[pltpu.VMEM((tm, D), F32)],
        compiler_params=_cparams("parallel"),
        name="ffn_final" if final else "ffn",
    )(x, g.reshape(1, D), w1, w3, w2, gf.reshape(1, D))


def _inproj_body(x_ref, g_ref, w_ref, ohy_ref, orw_ref, ogd_ref):
    h = _rms(x_ref[...], g_ref[...]).astype(BF16)
    c0, c1 = HY_COLS, HY_COLS + RW_COLS
    ohy_ref[...] = _dg(h, w_ref[:, 0:c0], NN)
    orw_ref[...] = _dg(h, w_ref[:, c0:c1], NN)
    ogd_ref[...] = _dg(h, w_ref[:, c1:c1 + GDN_COLS_PAD], NN)


def _inproj(x, g, w):
    T, D = x.shape
    tm = min(512, T)
    W = w.shape[1]
    widths = (HY_COLS, RW_COLS, GDN_COLS_PAD)
    return pl.pallas_call(
        _inproj_body,
        out_shape=tuple(jax.ShapeDtypeStruct((T, n), F32) for n in widths),
        grid=(T // tm,),
        in_specs=[pl.BlockSpec((tm, D), lambda i: (i, 0)), _full((1, D)), _full((D, W), True)],
        out_specs=tuple(pl.BlockSpec((tm, n), lambda i: (i, 0)) for n in widths),
        compiler_params=_cparams("parallel"),
        name="inproj",
    )(x, g.reshape(1, D), w)


def _hy_filter_body(z_ref, t_ref, freq_ref, w1_ref, b1_ref, w2_ref, b2_ref, w3_ref, dec_ref,
                    h_ref, s_ref):
    i = pl.program_id(0)
    freq = freq_ref[...]
    h = jnp.sin(freq * (_mm(z_ref[...], w1_ref[...], passes=3) + b1_ref[...]))
    h = jnp.sin(freq * (_mm(h, w2_ref[...], passes=3) + b2_ref[...]))
    h = _mm(h, w3_ref[...], passes=3) * jnp.exp(-t_ref[...] * dec_ref[...])
    C = D_HYENA
    fwd = h[:, :C]
    bwd = h[:, C:]
    rows = lax.broadcasted_iota(jnp.int32, bwd.shape, 0)
    bwd = jnp.where((rows == 0) & (i == 0), 0.0, bwd)
    h_ref[0] = fwd
    h_ref[1] = bwd
    part = jnp.sum(jnp.abs(fwd) + jnp.abs(bwd), axis=0, keepdims=True)

    @pl.when(i == 0)
    def _():
        s_ref[...] = part

    @pl.when(i > 0)
    def _():
        s_ref[...] += part


def _hy_filter(z, t, freq, w1, b1, w2, b2, w3, decay):
    L = z.shape[0]
    tl = min(1024, L)
    C = D_HYENA
    return pl.pallas_call(
        _hy_filter_body,
        out_shape=(jax.ShapeDtypeStruct((2, L, C), F32), jax.ShapeDtypeStruct((1, C), F32)),
        grid=(L // tl,),
        in_specs=[pl.BlockSpec((tl, LANES), lambda i: (i, 0)), pl.BlockSpec((tl, 1), lambda i: (i, 0)),
                  _full((1, HY_FFN)), _full((LANES, HY_FFN)), _full((1, HY_FFN)),
                  _full((HY_FFN, HY_FFN)), _full((1, HY_FFN)), _full((HY_FFN, 2 * C)), _full((1, 2 * C))],
        out_specs=(pl.BlockSpec((2, tl, C), lambda i: (0, i, 0)), _full((1, C))),
        compiler_params=_cparams("arbitrary"),
        name="hy_filter",
    )(z, t, freq.reshape(1, -1), w1, b1.reshape(1, -1), w2, b2.reshape(1, -1), w3, decay.reshape(1, -1))


def _hy_pre_body(p_ref, pp_ref, pn_ref, w_ref, b_ref, u_ref, x0_ref):
    p = p_ref[0]
    prev_row, next_row = _halo_rows(pp_ref, pn_ref)
    prev, nxt = _shift_rows(p, prev_row, next_row)
    c = w_ref[0:1, :] * prev + w_ref[1:2, :] * p + w_ref[2:3, :] * nxt + b_ref[...]
    C = D_HYENA
    x0_ref[0] = c[:, :C]
    u_ref[0] = c[:, C:2 * C] * c[:, 2 * C:]


def _hy_pre(p, w, b):
    B, L, W = p.shape
    tm = min(512, L)
    C = D_HYENA
    return pl.pallas_call(
        _hy_pre_body,
        out_shape=(jax.ShapeDtypeStruct((B, L, C), F32),) * 2,
        grid=(B, L // tm),
        in_specs=[*_halo_specs(tm, W, L), _full((3, W)), _full((1, W))],
        out_specs=(pl.BlockSpec((1, tm, C), lambda b, i: (b, i, 0)),) * 2,
        compiler_params=_cparams("parallel", "parallel"),
        name="hy_pre",
    )(p, p, p, w, b.reshape(1, W))


FFT_TJ = 16


def _fft_a_body(u_ref, m_ref, twr_ref, twi_ref, o_ref, acc_ref, *, na, tj):
    reps = u_ref.shape[-1] // LANES
    for j in range(tj):
        a = _mm_dft(m_ref[...], u_ref[0, :, j, :])
        ar, ai = a[:na], a[na:]
        twr = jnp.concatenate([twr_ref[j]] * reps, axis=1)
        twi = jnp.concatenate([twi_ref[j]] * reps, axis=1)
        acc_ref[0, :, j, :] = ar * twr - ai * twi
        acc_ref[1, :, j, :] = ar * twi + ai * twr
    o_ref[0] = acc_ref[...].astype(BF16)


def _fft_a(u4, m, twr, twi):
    B, half, nb, C = u4.shape
    na = 2 * half
    tj = FFT_TJ
    tw = pl.BlockSpec((tj, na, LANES), lambda b, j: (j, 0, 0))
    return pl.pallas_call(
        functools.partial(_fft_a_body, na=na, tj=tj),
        out_shape=jax.ShapeDtypeStruct((B, 2, na, nb, C), BF16),
        grid=(B, nb // tj),
        in_specs=[pl.BlockSpec((1, half, tj, C), lambda b, j: (b, 0, j, 0)), _full((2 * na, half)), tw, tw],
        out_specs=pl.BlockSpec((1, 2, na, tj, C), lambda b, j: (b, 0, 0, j, 0)),
        scratch_shapes=[pltpu.VMEM((2, na, tj, C), F32)],
        compiler_params=_cparams("parallel", "parallel"),
        name="fft_a",
    )(u4, m, twr, twi)


def _fft_filt_body(bf_ref, bb_ref, m_ref, s_ref, o_ref, *, tk):
    nb = FFT_NB
    inv = 1.0 / s_ref[...]
    for k in range(tk):
        xf = _mm_dft(m_ref[...], jnp.concatenate([bf_ref[0, 0, k], bf_ref[0, 1, k]], axis=0))
        xb = _mm_dft(m_ref[...], jnp.concatenate([bb_ref[0, 0, k], bb_ref[0, 1, k]], axis=0))
        o_ref[0, k] = (xf[:nb] + xb[:nb]) * inv
        o_ref[1, k] = (xf[nb:] - xb[nb:]) * inv


def _fft_filt(bt, m, s):
    _, _, na, nb, C = bt.shape
    tk = min(4, na)
    return pl.pallas_call(
        functools.partial(_fft_filt_body, tk=tk),
        out_shape=jax.ShapeDtypeStruct((2, na, nb, C), F32),
        grid=(na // tk,),
        in_specs=[pl.BlockSpec((1, 2, tk, nb, C), lambda i: (0, 0, i, 0, 0)),
                  pl.BlockSpec((1, 2, tk, nb, C), lambda i: (1, 0, i, 0, 0)),
                  _full((2 * nb, 2 * nb)), _full((1, C))],
        out_specs=pl.BlockSpec((2, tk, nb, C), lambda i: (0, i, 0, 0)),
        compiler_params=_cparams("parallel"),
        name="fft_filt",
    )(bt, bt, m, s)


def _fft_c_body(b_ref, k_ref, mf_ref, mi_ref, twr_ref, twi_ref, o_ref, *, tk):
    nb = FFT_NB
    for k in range(tk):
        x = _mm_dft(mf_ref[...], jnp.concatenate([b_ref[0, 0, k], b_ref[0, 1, k]], axis=0))
        xr, xi = x[:nb], x[nb:]
        kr, ki = k_ref[0, k], k_ref[1, k]
        y = jnp.concatenate([xr * kr - xi * ki, xr * ki + xi * kr], axis=0)
        d = _mm_dft(mi_ref[...], y)
        dr, di = d[:nb], d[nb:]
        reps = dr.shape[1] // LANES
        twr = jnp.concatenate([twr_ref[k]] * reps, axis=1)
        twi = jnp.concatenate([twi_ref[k]] * reps, axis=1)
        o_ref[0, 0, k] = (dr * twr + di * twi).astype(BF16)
        o_ref[0, 1, k] = (di * twr - dr * twi).astype(BF16)


def _fft_c(bt, khat, mf, mi, twr, twi):
    B, _, na, nb, C = bt.shape
    tk = min(8, na)
    blk = pl.BlockSpec((1, 2, tk, nb, C), lambda i, b: (b, 0, i, 0, 0))
    m = _full((2 * nb, 2 * nb))
    tw = pl.BlockSpec((tk, nb, LANES), lambda i, b: (i, 0, 0))
    return pl.pallas_call(
        functools.partial(_fft_c_body, tk=tk),
        out_shape=jax.ShapeDtypeStruct(bt.shape, BF16),
        grid=(na // tk, B),
        in_specs=[blk, pl.BlockSpec((2, tk, nb, C), lambda i, b: (0, i, 0, 0)), m, m, tw, tw],
        out_specs=blk,
        compiler_params=_cparams("parallel", "parallel"),
        name="fft_c",
    )(bt, khat, mf, mi, twr, twi)


def _fft_out_body(e_ref, m_ref, u_ref, x0_ref, bias_ref, o_ref, ef_ref, acc_ref, *, tj):
    ef_ref[...] = e_ref[0].astype(F32)
    for j in range(tj):
        e = jnp.concatenate([ef_ref[0, :, j, :], ef_ref[1, :, j, :]], axis=0)
        conv = _mm_dft(m_ref[...], e)
        acc_ref[:, j, :] = x0_ref[0, :, j, :] * (conv + bias_ref[...] * u_ref[0, :, j, :])
    o_ref[0] = acc_ref[...].astype(BF16)


def _fft_out(e, m, u4, x04, bias):
    B, half, nb, C = u4.shape
    na = 2 * half
    tj = FFT_TJ
    blk = pl.BlockSpec((1, half, tj, C), lambda b, j: (b, 0, j, 0))
    return pl.pallas_call(
        functools.partial(_fft_out_body, tj=tj),
        out_shape=jax.ShapeDtypeStruct(u4.shape, BF16),
        grid=(B, nb // tj),
        in_specs=[pl.BlockSpec((1, 2, na, tj, C), lambda b, j: (b, 0, 0, j, 0)), _full((half, 2 * na)), blk, blk,
                  _full((1, C))],
        out_specs=blk,
        scratch_shapes=[pltpu.VMEM((2, na, tj, C), F32), pltpu.VMEM((half, tj, C), F32)],
        compiler_params=_cparams("parallel", "parallel"),
        name="fft_out",
    )(e, m, u4, x04, bias)


@functools.lru_cache(maxsize=None)
def _fft_consts_np(L):
    n = 2 * L
    nb = FFT_NB
    na = n // nb
    half = na // 2
    ka = np.arange(na)
    fa = np.exp(-2j * np.pi * np.outer(ka, ka) / na)
    kb = np.arange(nb)
    fb = np.exp(-2j * np.pi * np.outer(kb, kb) / nb)
    m_a = np.concatenate([fa.real[:, :half], fa.imag[:, :half]], axis=0)
    m_f = np.block([[fb.real, -fb.imag], [fb.imag, fb.real]])
    m_i = np.block([[fb.real, fb.imag], [-fb.imag, fb.real]])
    m_o = np.concatenate([fa.real[:half, :], fa.imag[:half, :]], axis=1) / n
    tw = np.exp(-2j * np.pi * (np.outer(ka, kb) % n) / n)
    f32 = lambda a: np.asarray(a, np.float32)
    return f32(m_a), f32(m_f), f32(m_i), f32(m_o), f32(tw.real), f32(tw.imag)


def _hyena_pos(L):
    t = jnp.linspace(0.0, 1.0, L, dtype=F32)[:, None]
    ang = 2.0 * math.pi * jnp.arange(L, dtype=F32)[:, None] / L
    bands = jnp.linspace(1e-4, HY_BANDS - 1, HY_BANDS, dtype=F32)[None, :]
    z = jnp.concatenate([t, jnp.cos(bands * ang), -jnp.sin(bands * ang)], axis=-1)
    return jnp.pad(z, ((0, 0), (0, LANES - HY_EMB))), t


def _hyena(p_hy, z, t, prm, l):
    B, L, _ = p_hy.shape
    C = D_HYENA
    nb = FFT_NB
    na = 2 * L // nb
    half = na // 2
    m_a, m_f, m_i, m_o, twr, twi = _fft_consts_np(L)
    ma, mf, mi, mo = (jnp.asarray(m) for m in (m_a, m_f, m_i, m_o))
    twr_l = jnp.broadcast_to(jnp.asarray(twr)[:, :, None], (na, nb, LANES))
    twi_l = jnp.broadcast_to(jnp.asarray(twi)[:, :, None], (na, nb, LANES))
    twr_t = jnp.broadcast_to(jnp.asarray(twr.T)[:, :, None], (nb, na, LANES))
    twi_t = jnp.broadcast_to(jnp.asarray(twi.T)[:, :, None], (nb, na, LANES))

    w1 = jnp.pad(prm["hy_w1"][l], ((0, LANES - HY_EMB), (0, 0)))
    hraw, hsum = _hy_filter(z, t, prm["hy_freq"][l], w1, prm["hy_b1"][l], prm["hy_w2"][l],
                            prm["hy_b2"][l], prm["hy_w3"][l], prm["hy_decay"][l])
    khat = _fft_filt(_fft_a(hraw.reshape(2, half, nb, C), ma, twr_t, twi_t), mf, hsum)

    u, x0 = _hy_pre(p_hy, prm["hy_conv_w"][l], prm["hy_conv_b"][l])
    u4 = u.reshape(B, half, nb, C)
    e = _fft_c(_fft_a(u4, ma, twr_t, twi_t), khat, mf, mi, twr_l, twi_l)
    y = _fft_out(e, mo, u4, x0.reshape(B, half, nb, C), prm["hy_bias"][l].reshape(1, C))
    return y.reshape(B * L, C)


def _head_block_mask():
    r = lax.broadcasted_iota(jnp.int32, (PAIR, PAIR), 0)
    c = lax.broadcasted_iota(jnp.int32, (PAIR, PAIR), 1)
    return ((r // HEAD_DIM) == (c // HEAD_DIM)).astype(F32)


def _bd(x, hm):
    return jnp.concatenate([x, x], axis=0) * hm


def _cat_masks(reverse):
    r = lax.broadcasted_iota(jnp.int32, (CHUNK, PAIR), 0)
    s = lax.broadcasted_iota(jnp.int32, (CHUNK, PAIR), 1) % CHUNK
    if reverse:
        return s > r, s >= r, s == r
    return s < r, s <= r, s == r


NEUMANN_PASSES = 1
GRAM_PASSES = 1


def _neumann_inverse(a, diag, hm):
    C = CHUNK
    t = jnp.where(diag, 1.0, 0.0) + a
    p = _mm(a, _bd(a, hm), passes=NEUMANN_PASSES)
    yield
    steps = int(math.log2(C)) - 1
    for _ in range(steps - 1):
        pt = _mm(jnp.concatenate([p, t], axis=0), _bd(p, hm), passes=NEUMANN_PASSES)
        yield
        p, t = pt[:C], t + pt[C:]
    return t + _mm(t, _bd(p, hm), passes=NEUMANN_PASSES)


def _interleave(gens):
    results = [None] * len(gens)
    alive = list(range(len(gens)))
    while alive:
        for i in list(alive):
            try:
                next(gens[i])
            except StopIteration as done:
                results[i] = done.value
                alive.remove(i)
    return results


def _rw_prep_body(p_ref, pp_ref, pn_ref, mup_ref, mun_ref, wl_ref, w0_ref, al_ref, a0_ref, gl_ref,
                  kk_ref, ka_ref, rk_ref, hs_ref,
                  v_o, gate_o, bonus_o, atf_o, rtf_o, khf_o, bhf_o, atb_o, rtb_o, khb_o, bhb_o, ptf_o, ptb_o):
    p = p_ref[0]
    tm = p.shape[0]
    prev_row, next_row = _halo_rows(pp_ref, pn_ref)
    prev, nxt = _shift_rows(p, prev_row, next_row)
    mup, mun = mup_ref[...], mun_ref[...]
    p = (1.0 - mup - mun) * p + mup * prev + mun * nxt
    D = D_RWKV
    r, k, v = p[:, 0:D], p[:, D:2 * D], p[:, 2 * D:3 * D]
    o = 3 * D
    lw = (p[:, o:o + RW_LORA_W], p[:, o + RW_LORA_W:o + 2 * RW_LORA_W])
    o += 2 * RW_LORA_W
    la = (p[:, o:o + RW_LORA_A], p[:, o + RW_LORA_A:o + 2 * RW_LORA_A])
    o += 2 * RW_LORA_A
    lg = p[:, o:o + RW_LORA_G]
    hs = hs_ref[...]
    gate_o[0] = _mm(_sigmoid(lg), gl_ref[...], passes=3)
    k2 = k * kk_ref[...]
    kkn = k2 * lax.rsqrt(_head_stat(k2 * k2, hs) + L2_EPS)
    v_o[0] = v.astype(BF16)
    bonus_o[0] = _head_stat(r * k * rk_ref[...], hs) * v
    outs = ((atf_o, rtf_o, khf_o, bhf_o, ptf_o), (atb_o, rtb_o, khb_o, bhb_o, ptb_o))
    cb = min(CUM_ROWS, tm)
    for d in range(2):
        at_o, rt_o, kh_o, bh_o, pt_o = outs[d]
        logw = -RW_DECAY_SCALE * _sigmoid(w0_ref[d:d + 1, :] + _mm(jnp.tanh(lw[d]), wl_ref[d], passes=3))
        a = _sigmoid(a0_ref[d:d + 1, :] + _mm(la[d], al_ref[d], passes=3))
        kd = k * (1.0 + (a - 1.0) * ka_ref[...])
        sel = jnp.concatenate([_chunk_cum_matrix(cb, bool(d)), _chunk_total_matrix(cb)], axis=0)
        for sb in range(tm // cb):
            rows = slice(sb * cb, (sb + 1) * cb)
            lw_s = logw[rows]
            ct = _mm_lconst(sel, lw_s)
            cum = ct[:cb]
            pinv = jnp.exp(-cum)
            at_o[0, rows, :] = (-kkn[rows] * jnp.exp(cum - lw_s)).astype(BF16)
            rt_o[0, rows, :] = (r[rows] * jnp.exp(cum)).astype(BF16)
            kh_o[0, rows, :] = (kd[rows] * pinv).astype(BF16)
            bh_o[0, rows, :] = (kkn[rows] * a[rows] * pinv).astype(BF16)
            pt_o[0, sb * cb // SUBLANES:(sb + 1) * cb // SUBLANES, :] = jnp.exp(ct[cb:])


CUM_ROWS = 256


def _chunk_cum_matrix(n, reverse):
    r = lax.broadcasted_iota(jnp.int32, (n, n), 0)
    c = lax.broadcasted_iota(jnp.int32, (n, n), 1)
    same = (r // CHUNK) == (c // CHUNK)
    return (same & ((c >= r) if reverse else (c <= r))).astype(F32)


def _chunk_total_matrix(n):
    r = lax.broadcasted_iota(jnp.int32, (n // SUBLANES, n), 0)
    c = lax.broadcasted_iota(jnp.int32, (n // SUBLANES, n), 1)
    return (r // (CHUNK // SUBLANES) == c // CHUNK).astype(F32)


def _rw_prep(p, prm, l):
    B, L, W = p.shape
    tm = min(512, L)
    D = D_RWKV
    row = lambda a: a.reshape(1, -1)
    out = pl.BlockSpec((1, tm, D), lambda b, i: (b, i, 0))
    tot = pl.BlockSpec((1, tm // SUBLANES, D), lambda b, i: (b, i, 0))
    full = jax.ShapeDtypeStruct((B, L, D), F32)
    half = jax.ShapeDtypeStruct((B, L, D), BF16)
    small = jax.ShapeDtypeStruct((B, L // SUBLANES, D), F32)
    return pl.pallas_call(
        _rw_prep_body,
        out_shape=(half, full, full) + (half,) * 8 + (small,) * 2,
        grid=(B, L // tm),
        in_specs=[*_halo_specs(tm, W, L), _full((1, W)), _full((1, W)),
                  _full((2, RW_LORA_W, D)), _full((2, D)), _full((2, RW_LORA_A, D)), _full((2, D)),
                  _full((RW_LORA_G, D)), _full((1, D)), _full((1, D)), _full((1, D)), _full((D, D))],
        out_specs=(out,) * 11 + (tot,) * 2,
        compiler_params=_cparams("parallel", "parallel"),
        name="rw_prep",
    )(p, p, p, row(prm["rw_mu_prev"][l]), row(prm["rw_mu_next"][l]), prm["rw_w_lora"][l], prm["rw_w0"][l],
      prm["rw_a_lora"][l], prm["rw_a0"][l], prm["rw_g_lora"][l], row(prm["rw_k_k"][l]),
      row(prm["rw_k_a"][l]), row(prm["rw_r_k"][l]), _head_sum_matrix(D, RW_HEAD_DIM))


def _rw_chunk(v, at, rt, kh, bh, ptot, s_in, s_out, reverse, hm):
    C = CHUNK
    strict, incl, diag = _cat_masks(reverse)
    kb_bd = jnp.concatenate([_bd(kh, hm), _bd(bh, hm)], axis=0)
    g = _mm(jnp.concatenate([at, rt], axis=0), kb_bd, NT, passes=GRAM_PASSES)
    yield
    n = 2 * C
    a_ak = jnp.where(strict, g[:C, :n], 0.0)
    a_ab = jnp.where(strict, g[:C, n:], 0.0)
    a_rk = jnp.where(incl, g[C:, :n], 0.0)
    a_rb = jnp.where(incl, g[C:, n:], 0.0)
    av = _mm(jnp.concatenate([a_ak, a_rk], axis=0), _bd(v, hm))
    t = yield from _neumann_inverse(a_ab, diag, hm)
    yield
    uw = _mm(t, jnp.concatenate([_bd(av[:C], hm), _bd(at, hm)], axis=1))
    yield
    while s_in[0] is None:
        yield
    s0 = s_in[0]
    ws = _mm(jnp.concatenate([uw[:, PAIR:], rt], axis=0), s0, NT)
    yield
    u = uw[:, :PAIR] + ws[:C]
    upd = _mm(jnp.concatenate([v, u], axis=0), jnp.concatenate([kh, bh], axis=0), TN) * hm
    s_out[0] = (s0 + upd) * ptot
    return ws[C:] + av[C:] + _mm(a_rb, _bd(u, hm))


def _scan_schedule(n_pairs, make_chain, s_ref):
    gens, outs, finals = [], [], []
    for c in range(CHUNKS_PER_STEP):
        for d in range(2):
            ci = CHUNKS_PER_STEP - 1 - c if d else c
            rows = slice(ci * CHUNK, (ci + 1) * CHUNK)
            for j in range(n_pairs):
                if c == 0:
                    finals.append([s_ref[d, j]])
                k = d * n_pairs + j
                s_in = finals[k]
                s_out = [None]
                finals[k] = s_out
                gens.append(make_chain(d, j, rows, s_in, s_out))
                outs.append((d, j, rows))
    return gens, outs, finals


def _rw_scan_body(vf, atf, rtf, khf, bhf, ptf, vb, atb, rtb, khb, bhb, ptb, yf_o, yb_o, s_ref):
    @pl.when(pl.program_id(1) == 0)
    def _():
        s_ref[...] = jnp.zeros(s_ref.shape, F32)

    hm = _head_block_mask()
    dirs = ((vf, atf, rtf, khf, bhf, ptf, yf_o), (vb, atb, rtb, khb, bhb, ptb, yb_o))
    n_pairs = D_RWKV // PAIR

    def make_chain(d, j, rows, s_in, s_out):
        sl = slice(j * PAIR, (j + 1) * PAIR)
        v, at, rt, kh, bh, pt, _ = dirs[d]
        t0 = rows.start // SUBLANES
        return _rw_chunk(v[0, rows, sl], at[0, rows, sl], rt[0, rows, sl], kh[0, rows, sl], bh[0, rows, sl],
                         pt[0, t0:t0 + 1, sl], s_in, s_out, bool(d), hm)

    gens, outs, finals = _scan_schedule(n_pairs, make_chain, s_ref)
    for y, (d, j, rows) in zip(_interleave(gens), outs):
        dirs[d][-1][0, rows, j * PAIR:(j + 1) * PAIR] = y
    for k, cell in enumerate(finals):
        s_ref[k // n_pairs, k % n_pairs] = cell[0]


def _rw_scan(v, atf, rtf, khf, bhf, ptf, atb, rtb, khb, bhb, ptb):
    B, L, D = v.shape
    blk = CHUNKS_PER_STEP * CHUNK
    n = L // blk
    fw = pl.BlockSpec((1, blk, D), lambda b, i: (b, i, 0))
    bw = pl.BlockSpec((1, blk, D), lambda b, i: (b, n - 1 - i, 0))
    fwt = pl.BlockSpec((1, blk // SUBLANES, D), lambda b, i: (b, i, 0))
    bwt = pl.BlockSpec((1, blk // SUBLANES, D), lambda b, i: (b, n - 1 - i, 0))
    return pl.pallas_call(
        _rw_scan_body,
        out_shape=(jax.ShapeDtypeStruct((B, L, D), F32),) * 2,
        grid=(B, n),
        in_specs=[fw] * 5 + [fwt] + [bw] * 5 + [bwt],
        out_specs=(fw, bw),
        scratch_shapes=[pltpu.VMEM((2, D // PAIR, PAIR, PAIR), F32)],
        compiler_params=_cparams("parallel", "arbitrary"),
        name="rw_scan",
    )(v, atf, rtf, khf, bhf, ptf, v, atb, rtb, khb, bhb, ptb)


GDN_G_LANE = 0
GDN_BETA_LANE = 2 * GDN_HEADS


def _gdn_prep_body(p_ref, pp_ref, pn_ref, s_ref, w_ref, alog_ref, dt_ref, hs_ref,
                   q_o, k_o, v_o, gcf_o, gcb_o, bf_o, bb_o):
    p = p_ref[0]
    tm = p.shape[0]
    prev_row, next_row = _halo_rows(pp_ref, pn_ref)
    prev, nxt = _shift_rows(p, prev_row, next_row)
    c = _silu(w_ref[0:1, :] * prev + w_ref[1:2, :] * p + w_ref[2:3, :] * nxt)
    D = D_GDN
    q, k, v = c[:, 0:D], c[:, D:2 * D], c[:, 2 * D:3 * D]
    hs = hs_ref[...]
    q_o[0] = q * lax.rsqrt(_head_stat(q * q, hs) + L2_EPS) * (GDN_HEAD_DIM ** -0.5)
    k_o[0] = k * lax.rsqrt(_head_stat(k * k, hs) + L2_EPS)
    v_o[0] = v
    s = s_ref[0]
    lane = lax.broadcasted_iota(jnp.int32, s.shape, 1)
    g = -jnp.exp(alog_ref[...]) * _softplus(s + dt_ref[...])
    gs = jnp.where(lane < GDN_BETA_LANE, g, jnp.where(lane < 4 * GDN_HEADS, _sigmoid(s), 0.0))
    er = lax.broadcasted_iota(jnp.int32, (LANES, 4 * D), 0)
    ec = lax.broadcasted_iota(jnp.int32, (LANES, 4 * D), 1)
    expand = (er == (ec // D) * GDN_HEADS + (ec % D) // GDN_HEAD_DIM).astype(F32)
    cb = min(CUM_ROWS, tm)
    lane_s = lax.broadcasted_iota(jnp.int32, (cb, LANES), 1)
    for sb in range(tm // cb):
        rows = slice(sb * cb, (sb + 1) * cb)
        g_s = gs[rows]
        gc = jnp.where(lane_s < GDN_HEADS, _mm_lconst(_chunk_cum_matrix(cb, False), g_s),
                       jnp.where(lane_s < GDN_BETA_LANE, _mm_lconst(_chunk_cum_matrix(cb, True), g_s), g_s))
        gx = _mm_rconst(gc, expand)
        gcf_o[0, rows, :] = gx[:, 0:D]
        gcb_o[0, rows, :] = gx[:, D:2 * D]
        bf_o[0, rows, :] = gx[:, 2 * D:3 * D]
        bb_o[0, rows, :] = gx[:, 3 * D:]


def _gdn_prep(p, prm, l):
    B, L, W = p.shape
    tm = min(512, L)
    D = D_GDN
    Wq = 3 * D
    pad = LANES - 2 * GDN_HEADS
    alog = jnp.pad(prm["gdn_a_log"][l].reshape(-1), (0, pad)).reshape(1, LANES)
    dt = jnp.pad(prm["gdn_dt_bias"][l].reshape(-1), (0, pad)).reshape(1, LANES)
    out = pl.BlockSpec((1, tm, D), lambda b, i: (b, i, 0))
    side = pl.BlockSpec((1, tm, LANES), lambda b, i: (b, i, 4 * D // LANES))
    return pl.pallas_call(
        _gdn_prep_body,
        out_shape=(jax.ShapeDtypeStruct((B, L, D), F32),) * 7,
        grid=(B, L // tm),
        in_specs=[*_halo_specs(tm, Wq, L), side, _full((3, Wq)), _full((1, LANES)), _full((1, LANES)),
                  _full((D, D))],
        out_specs=(out,) * 7,
        compiler_params=_cparams("parallel", "parallel"),
        name="gdn_prep",
    )(p, p, p, p, prm["gdn_conv_w"][l], alog, dt, _head_sum_matrix(D, GDN_HEAD_DIM))


def _gdn_chunk(q, k, v, gcx, beta, s_in, s_out, reverse, hm):
    C = CHUNK
    strict, incl, diag = _cat_masks(reverse)
    grow = _mm_lconst(jnp.ones((C, C), F32), jnp.where(diag, gcx, 0.0))
    kb = k * beta
    kq = _mm(jnp.concatenate([kb, q], axis=0), _bd(k, hm), NT, passes=GRAM_PASSES)
    yield
    dec = jnp.exp(jnp.where(incl, gcx - grow, -1e30))
    a = jnp.where(strict, kq[:C] * dec, 0.0)
    attn = kq[C:] * dec
    t = yield from _neumann_inverse(-a, diag, hm)
    yield
    egc = jnp.exp(gcx)
    sol = _mm(t, jnp.concatenate([_bd(v * beta, hm), _bd(kb * egc, hm)], axis=1))
    yield
    while s_in[0] is None:
        yield
    s0 = s_in[0]
    ws = _mm(jnp.concatenate([sol[:, PAIR:], q * egc], axis=0), s0)
    yield
    v_new = sol[:, :PAIR] - ws[:C]
    glast = gcx[0:1] if reverse else gcx[C - 1:C]
    upd = _mm(k * jnp.exp(glast - gcx), v_new, TN) * hm
    s_out[0] = s0 * jnp.exp(glast) + upd
    return ws[C:] + _mm(attn, _bd(v_new, hm))


def _gdn_scan_body(qf, kf, vf, gf, btf, qb, kb, vb, gb, btb, of_o, ob_o, s_ref):
    @pl.when(pl.program_id(1) == 0)
    def _():
        s_ref[...] = jnp.zeros(s_ref.shape, F32)

    hm = _head_block_mask()
    dirs = ((qf, kf, vf, gf, btf, of_o), (qb, kb, vb, gb, btb, ob_o))
    n_pairs = D_GDN // PAIR

    def make_chain(d, j, rows, s_in, s_out):
        sl = slice(j * PAIR, (j + 1) * PAIR)
        q, k, v, g, bt, _ = dirs[d]
        return _gdn_chunk(q[0, rows, sl], k[0, rows, sl], v[0, rows, sl], g[0, rows, sl], bt[0, rows, sl],
                          s_in, s_out, bool(d), hm)

    gens, outs, finals = _scan_schedule(n_pairs, make_chain, s_ref)
    for o, (d, j, rows) in zip(_interleave(gens), outs):
        dirs[d][-1][0, rows, j * PAIR:(j + 1) * PAIR] = o
    for k, cell in enumerate(finals):
        s_ref[k // n_pairs, k % n_pairs] = cell[0]


def _gdn_scan(q, k, v, gcf, gcb, btf, btb):
    B, L, D = q.shape
    blk = CHUNKS_PER_STEP * CHUNK
    n = L // blk
    fw = pl.BlockSpec((1, blk, D), lambda b, i: (b, i, 0))
    bw = pl.BlockSpec((1, blk, D), lambda b, i: (b, n - 1 - i, 0))
    return pl.pallas_call(
        _gdn_scan_body,
        out_shape=(jax.ShapeDtypeStruct((B, L, D), F32),) * 2,
        grid=(B, n),
        in_specs=[fw] * 5 + [bw] * 5,
        out_specs=(fw, bw),
        scratch_shapes=[pltpu.VMEM((2, D // PAIR, PAIR, PAIR), F32)],
        compiler_params=_cparams("parallel", "arbitrary"),
        name="gdn_scan",
    )(q, k, v, gcf, btf, q, k, v, gcb, btb)


def _mix_out_body(x_ref, yhy_ref, yf_ref, yb_ref, gate_ref, bonus_ref, of_ref, ob_ref, zg_ref,
                  gnw_ref, gnb_ref, nw_ref, avg_ref, w_ref, o_ref):
    avg = avg_ref[...]
    y = yf_ref[...] + yb_ref[...]
    mu = _head_stat(y, avg)
    dlt = y - mu
    var = _head_stat(dlt * dlt, avg)
    y_rw = (dlt * lax.rsqrt(var + RW_GN_EPS) * gnw_ref[...] + gnb_ref[...] + bonus_ref[...]) * gate_ref[...]
    o = of_ref[...] + ob_ref[...]
    ms = _head_stat(o * o, avg)
    y_gdn = o * lax.rsqrt(ms + NORM_EPS) * nw_ref[...] * _silu(zg_ref[...])
    c0, c1 = D_HYENA, D_HYENA + D_RWKV
    acc = _dg(yhy_ref[...].astype(BF16), w_ref[0:c0, :], NN)
    acc += _dg(y_rw.astype(BF16), w_ref[c0:c1, :], NN)
    acc += _dg(y_gdn.astype(BF16), w_ref[c1:, :], NN)
    o_ref[...] = x_ref[...] + acc


def _mix_out(x, y_hy, yf, yb, gate, bonus, of, ob, p_gdn, prm, l, w_out):
    T, D = x.shape
    tm = min(512, T)
    Dh = D_RWKV
    row = lambda a: a.reshape(1, -1)
    tile = lambda n: pl.BlockSpec((tm, n), lambda i: (i, 0))
    zg = pl.BlockSpec((tm, D_GDN), lambda i: (i, 3))
    avg = _head_sum_matrix(Dh, HEAD_DIM, 1.0 / HEAD_DIM)
    return pl.pallas_call(
        _mix_out_body,
        out_shape=jax.ShapeDtypeStruct((T, D), F32),
        grid=(T // tm,),
        in_specs=[tile(D), tile(D_HYENA)] + [tile(Dh)] * 6 + [zg, _full((1, Dh)), _full((1, Dh)),
                  _full((1, Dh)), _full((Dh, Dh)), _full((D, D))],
        out_specs=tile(D),
        compiler_params=_cparams("parallel"),
        name="mix_out",
    )(x, y_hy, yf, yb, gate, bonus, of, ob, p_gdn, row(prm["rw_gn_w"][l]), row(prm["rw_gn_b"][l]),
      row(jnp.tile(prm["gdn_norm_w"][l], GDN_HEADS)), avg, w_out)


def _mem_kv_body(m_ref, g_ref, wk_ref, wv_ref, k_o, v_o):
    h = _rms(m_ref[0], g_ref[...]).astype(BF16)
    k_o[0] = _dg(h, wk_ref[...], NN).astype(BF16)
    v_o[0] = _dg(h, wv_ref[...], NN).astype(BF16)


def _mem_kv(mem, g, wk, wv):
    B, M, D = mem.shape
    blk = pl.BlockSpec((1, M, D), lambda b: (b, 0, 0))
    return pl.pallas_call(
        _mem_kv_body,
        out_shape=(jax.ShapeDtypeStruct((B, M, D), BF16),) * 2,
        grid=(B,),
        in_specs=[blk, _full((1, D)), _full((D, D)), _full((D, D))],
        out_specs=(blk, blk),
        compiler_params=_cparams("parallel"),
        name="mem_kv",
    )(mem, g.reshape(1, D), wk, wv)


def _xattn_body(x_ref, g_ref, wq_ref, k_ref, v_ref, wo_ref, o_ref):
    x = x_ref[0]
    h = _rms(x, g_ref[...]).astype(BF16)
    q = (_dg(h, wq_ref[...], NN) * (XA_HEAD_DIM ** -0.5)).astype(BF16)
    outs = []
    for hd in range(XA_HEADS):
        sl = slice(hd * XA_HEAD_DIM, (hd + 1) * XA_HEAD_DIM)
        s = _dg(q[:, sl], k_ref[0, :, sl], NT)
        s = s - jnp.max(s, axis=-1, keepdims=True)
        e = jnp.exp(s)
        pr = e / jnp.sum(e, axis=-1, keepdims=True)
        outs.append(_dg(pr.astype(BF16), v_ref[0, :, sl], NN))
    o = jnp.concatenate(outs, axis=1).astype(BF16)
    o_ref[0] = x + _dg(o, wo_ref[...], NN)


def _xattn(x, g, wq, k, v, wo):
    B, L, D = x.shape
    M = k.shape[1]
    tm = min(512, L)
    tile = pl.BlockSpec((1, tm, D), lambda b, i: (b, i, 0))
    kv = pl.BlockSpec((1, M, D), lambda b, i: (b, 0, 0))
    return pl.pallas_call(
        _xattn_body,
        out_shape=jax.ShapeDtypeStruct((B, L, D), F32),
        grid=(B, L // tm),
        in_specs=[tile, _full((1, D)), _full((D, D)), kv, kv, _full((D, D))],
        out_specs=tile,
        compiler_params=_cparams("parallel", "parallel"),
        name="xattn",
    )(x, g.reshape(1, D), wq, k, v, wo)


def kernel(x, mem, norm_ffn1, ffn1_w1, ffn1_w3, ffn1_w2, norm_mix, w_in, w_out, hy_conv_w, hy_conv_b, hy_freq, hy_w1, hy_b1, hy_w2, hy_b2, hy_w3, hy_decay, hy_bias, rw_mu_prev, rw_mu_next, rw_w_lora, rw_w0, rw_a_lora, rw_a0, rw_g_lora, rw_k_k, rw_k_a, rw_r_k, rw_gn_w, rw_gn_b, gdn_conv_w, gdn_a_log, gdn_dt_bias, gdn_norm_w, norm_xattn, xa_wq, xa_wk, xa_wv, xa_wo, mem_norm, norm_ffn2, ffn2_w1, ffn2_w3, ffn2_w2, norm_final):
    prm = dict(hy_conv_w=hy_conv_w, hy_conv_b=hy_conv_b, hy_freq=hy_freq, hy_w1=hy_w1, hy_b1=hy_b1,
               hy_w2=hy_w2, hy_b2=hy_b2, hy_w3=hy_w3, hy_decay=hy_decay, hy_bias=hy_bias,
               rw_mu_prev=rw_mu_prev, rw_mu_next=rw_mu_next, rw_w_lora=rw_w_lora, rw_w0=rw_w0,
               rw_a_lora=rw_a_lora, rw_a0=rw_a0, rw_g_lora=rw_g_lora, rw_k_k=rw_k_k, rw_k_a=rw_k_a,
               rw_r_k=rw_r_k, rw_gn_w=rw_gn_w, rw_gn_b=rw_gn_b, gdn_conv_w=gdn_conv_w,
               gdn_a_log=gdn_a_log, gdn_dt_bias=gdn_dt_bias, gdn_norm_w=gdn_norm_w)
    B, L, D = x.shape
    depth = norm_ffn1.shape[0]
    T = B * L
    z_pos, t_pos = _hyena_pos(L)
    bf = lambda w: w.astype(BF16)
    w_in_p = jnp.pad(w_in, ((0, 0), (0, 0), (0, GDN_COLS_PAD - GDN_COLS)))
    xt = x.reshape(T, D)
    for l in range(depth):
        xt = _ffn(xt, norm_ffn1[l], bf(ffn1_w1[l]), bf(ffn1_w3[l]), bf(ffn1_w2[l]), norm_final, False)
        p_hy, p_rw, p_gdn = _inproj(xt, norm_mix[l], bf(w_in_p[l]))
        y_hy = _hyena(p_hy.reshape(B, L, -1), z_pos, t_pos, prm, l)
        v, gate, bonus, *rw_ops = _rw_prep(p_rw.reshape(B, L, -1), prm, l)
        yf, yb = _rw_scan(v, *rw_ops[0:4], rw_ops[8], *rw_ops[4:8], rw_ops[9])
        of, ob = _gdn_scan(*_gdn_prep(p_gdn.reshape(B, L, -1), prm, l))
        flat = lambda a: a.reshape(T, -1)
        xt = _mix_out(xt, y_hy, flat(yf), flat(yb), flat(gate), flat(bonus), flat(of), flat(ob), p_gdn,
                      prm, l, bf(w_out[l]))
        km, vm = _mem_kv(mem, mem_norm, bf(xa_wk[l]), bf(xa_wv[l]))
        xt = _xattn(xt.reshape(B, L, D), norm_xattn[l], bf(xa_wq[l]), km, vm, bf(xa_wo[l])).reshape(T, D)
        xt = _ffn(xt, norm_ffn2[l], bf(ffn2_w1[l]), bf(ffn2_w3[l]), bf(ffn2_w2[l]), norm_final,
                  l == depth - 1)
    return xt.reshape(B, L, D)
```

```python
import functools
import math

import numpy as np
import jax
import jax.numpy as jnp
from jax import lax
from jax.experimental import pallas as pl
from jax.experimental.pallas import tpu as pltpu

F32 = jnp.float32
BF16 = jnp.bfloat16

D_MODEL = 1024
D_HYENA = 256
RW_HEADS = 6
RW_HEAD_DIM = 64
D_RWKV = RW_HEADS * RW_HEAD_DIM
GDN_HEADS = 6
GDN_HEAD_DIM = 64
D_GDN = GDN_HEADS * GDN_HEAD_DIM
HY_BANDS = 16
HY_EMB = 1 + 2 * HY_BANDS
HY_FFN = 64
RW_LORA_W = 64
RW_LORA_A = 64
RW_LORA_G = 128
RW_DECAY_SCALE = 0.606531
RW_GN_EPS = 64e-5
XA_HEADS = 4
XA_HEAD_DIM = D_MODEL // XA_HEADS
D_FF = 2816
NORM_EPS = 1e-6
L2_EPS = 1e-6
HY_COLS = 3 * D_HYENA
RW_COLS = 3 * D_RWKV + 2 * RW_LORA_W + 2 * RW_LORA_A + RW_LORA_G
GDN_COLS = 4 * D_GDN + 4 * GDN_HEADS

LANES = 128
SUBLANES = 8
MXU_WIDTH = 256
VMEM_LIMIT_BYTES = 56 * 1024 * 1024
GDN_COLS_PAD = 4 * D_GDN + LANES
CHUNK = 64
CHUNKS_PER_STEP = 8
HEAD_DIM = 64
PAIR = 2 * HEAD_DIM
FFT_NB = LANES


def _cparams(*sem):
    return pltpu.CompilerParams(dimension_semantics=sem, vmem_limit_bytes=VMEM_LIMIT_BYTES)


def _full(shape, single=False):
    nd = len(shape)
    if single:
        return pl.BlockSpec(shape, lambda *_: (0,) * nd, pipeline_mode=pl.Buffered(1))
    return pl.BlockSpec(shape, lambda *_: (0,) * nd)


NN = (((1,), (0,)), ((), ()))
NT = (((1,), (1,)), ((), ()))
TN = (((0,), (0,)), ((), ()))


def _dg(a, b, dims):
    return lax.dot_general(a, b, dims, preferred_element_type=F32)


def _split2(x):
    hi = x.astype(BF16)
    lo = (x - hi.astype(F32)).astype(BF16)
    return hi, lo


def _mm(a, b, dims=NN, passes=1):
    if passes == 1:
        return _dg(a.astype(BF16), b.astype(BF16), dims)
    ah, al = _split2(a)
    bh, bl = _split2(b)
    return _dg(ah, bh, dims) + (_dg(al, bh, dims) + _dg(ah, bl, dims))


def _mm_lconst(c, x):
    n = x.shape[1]
    y = _dg(c.astype(BF16), jnp.concatenate(_split2(x), axis=1), NN)
    return y[:, :n] + y[:, n:]


def _mm_rconst(x, c):
    m = x.shape[0]
    y = _dg(jnp.concatenate(_split2(x), axis=0), c.astype(BF16), NN)
    return y[:m] + y[m:]


def _head_stat(x, c):
    return _dg(x.astype(BF16), c.astype(BF16), NN)


def _mm_dft(m, x):
    return _dg(m.astype(BF16), x.astype(BF16), NN)


def _rms(x, g):
    return x * lax.rsqrt(jnp.mean(x * x, axis=-1, keepdims=True) + NORM_EPS) * g


def _sigmoid(x):
    return 0.5 * jnp.tanh(0.5 * x) + 0.5


def _silu(x):
    return x * _sigmoid(x)


def _softplus(x):
    return jnp.maximum(x, 0.0) + jnp.log(1.0 + jnp.exp(-jnp.abs(x)))


def _shift_rows(p, prev_row, next_row):
    n = p.shape[0]
    rows = lax.broadcasted_iota(jnp.int32, p.shape, 0)
    prev = jnp.where(rows == 0, prev_row, pltpu.roll(p, 1, 0))
    nxt = jnp.where(rows == n - 1, next_row, pltpu.roll(p, n - 1, 0))
    return prev, nxt


def _halo_specs(tm, width, L, col_block=0):
    r = tm // SUBLANES
    last = L // SUBLANES - 1

    def cur(b, i):
        return (b, i, col_block)

    def prev(b, i):
        return (b, jnp.maximum(i * r - 1, 0), col_block)

    def nxt(b, i):
        return (b, jnp.minimum((i + 1) * r, last), col_block)

    return (pl.BlockSpec((1, tm, width), cur),
            pl.BlockSpec((1, SUBLANES, width), prev),
            pl.BlockSpec((1, SUBLANES, width), nxt))


def _halo_rows(prev_ref, next_ref):
    i = pl.program_id(1)
    n = pl.num_programs(1)
    prev_row = jnp.where(i > 0, prev_ref[0, SUBLANES - 1:SUBLANES, :], 0.0)
    next_row = jnp.where(i < n - 1, next_ref[0, 0:1, :], 0.0)
    return prev_row, next_row


def _head_sum_matrix(width, head_dim, scale=1.0):
    idx = np.arange(width) // head_dim
    return jnp.asarray((idx[:, None] == idx[None, :]).astype(np.float32) * scale)


def _ffn_body(x_ref, g_ref, w1_ref, w3_ref, w2_ref, gf_ref, o_ref, acc_ref, *, n_chunks, tf, final):
    x = x_ref[...]
    h = _rms(x, g_ref[...]).astype(BF16)
    for j in range(n_chunks):
        sl = slice(j * tf, (j + 1) * tf)
        a = _dg(h, w1_ref[:, sl], NN)
        b = _dg(h, w3_ref[:, sl], NN)
        t = (_silu(a) * b).astype(BF16)
        part = _dg(t, w2_ref[sl, :], NN)
        if j == 0:
            acc_ref[...] = part
        else:
            acc_ref[...] += part
    y = x + 0.5 * acc_ref[...]
    if final:
        y = _rms(y, gf_ref[...])
    o_ref[...] = y


def _ffn(x, g, w1, w3, w2, gf, final):
    T, D = x.shape
    FF = w1.shape[1]
    tm = min(1024, T)
    tf = MXU_WIDTH
    body = functools.partial(_ffn_body, n_chunks=FF // tf, tf=tf, final=final)
    return pl.pallas_call(
        body,
        out_shape=jax.ShapeDtypeStruct((T, D), F32),
        grid=(T // tm,),
        in_specs=[pl.BlockSpec((tm, D), lambda i: (i, 0)), _full((1, D)),
                  _full((D, FF), True), _full((D, FF), True), _full((FF, D), True), _full((1, D))],
        out_specs=pl.BlockSpec((tm, D), lambda i: (i, 0)),
        scratch_shapes=[pltpu.VMEM((tm, D), F32)],
        compiler_params=_cparams("parallel"),
        name="ffn_final" if final else "ffn",
    )(x, g.reshape(1, D), w1, w3, w2, gf.reshape(1, D))


def _inproj_body(x_ref, g_ref, w_ref, ohy_ref, orw_ref, ogd_ref):
    h = _rms(x_ref[...], g_ref[...]).astype(BF16)
    c0, c1 = HY_COLS, HY_COLS + RW_COLS
    ohy_ref[...] = _dg(h, w_ref[:, 0:c0], NN)
    orw_ref[...] = _dg(h, w_ref[:, c0:c1], NN)
    ogd_ref[...] = _dg(h, w_ref[:, c1:c1 + GDN_COLS_PAD], NN)


def _inproj(x, g, w):
    T, D = x.shape
    tm = min(512, T)
    W = w.shape[1]
    widths = (HY_COLS, RW_COLS, GDN_COLS_PAD)
    return pl.pallas_call(
        _inproj_body,
        out_shape=tuple(jax.ShapeDtypeStruct((T, n), F32) for n in widths),
        grid=(T // tm,),
        in_specs=[pl.BlockSpec((tm, D), lambda i: (i, 0)), _full((1, D)), _full((D, W), True)],
        out_specs=tuple(pl.BlockSpec((tm, n), lambda i: (i, 0)) for n in widths),
        compiler_params=_cparams("parallel"),
        name="inproj",
    )(x, g.reshape(1, D), w)


def _hy_filter_body(z_ref, t_ref, freq_ref, w1_ref, b1_ref, w2_ref, b2_ref, w3_ref, dec_ref,
                    h_ref, s_ref):
    i = pl.program_id(0)
    freq = freq_ref[...]
    h = jnp.sin(freq * (_mm(z_ref[...], w1_ref[...], passes=3) + b1_ref[...]))
    h = jnp.sin(freq * (_mm(h, w2_ref[...], passes=3) + b2_ref[...]))
    h = _mm(h, w3_ref[...], passes=3) * jnp.exp(-t_ref[...] * dec_ref[...])
    C = D_HYENA
    fwd = h[:, :C]
    bwd = h[:, C:]
    rows = lax.broadcasted_iota(jnp.int32, bwd.shape, 0)
    bwd = jnp.where((rows == 0) & (i == 0), 0.0, bwd)
    h_ref[0] = fwd
    h_ref[1] = bwd
    part = jnp.sum(jnp.abs(fwd) + jnp.abs(bwd), axis=0, keepdims=True)

    @pl.when(i == 0)
    def _():
        s_ref[...] = part

    @pl.when(i > 0)
    def _():
        s_ref[...] += part


def _hy_filter(z, t, freq, w1, b1, w2, b2, w3, decay):
    L = z.shape[0]
    tl = min(1024, L)
    C = D_HYENA
    return pl.pallas_call(
        _hy_filter_body,
        out_shape=(jax.ShapeDtypeStruct((2, L, C), F32), jax.ShapeDtypeStruct((1, C), F32)),
        grid=(L // tl,),
        in_specs=[pl.BlockSpec((tl, LANES), lambda i: (i, 0)), pl.BlockSpec((tl, 1), lambda i: (i, 0)),
                  _full((1, HY_FFN)), _full((LANES, HY_FFN)), _full((1, HY_FFN)),
                  _full((HY_FFN, HY_FFN)), _full((1, HY_FFN)), _full((HY_FFN, 2 * C)), _full((1, 2 * C))],
        out_specs=(pl.BlockSpec((2, tl, C), lambda i: (0, i, 0)), _full((1, C))),
        compiler_params=_cparams("arbitrary"),
        name="hy_filter",
    )(z, t, freq.reshape(1, -1), w1, b1.reshape(1, -1), w2, b2.reshape(1, -1), w3, decay.reshape(1, -1))


def _hy_pre_body(p_ref, pp_ref, pn_ref, w_ref, b_ref, u_ref, x0_ref):
    p = p_ref[0]
    prev_row, next_row = _halo_rows(pp_ref, pn_ref)
    prev, nxt = _shift_rows(p, prev_row, next_row)
    c = w_ref[0:1, :] * prev + w_ref[1:2, :] * p + w_ref[2:3, :] * nxt + b_ref[...]
    C = D_HYENA
    x0_ref[0] = c[:, :C]
    u_ref[0] = c[:, C:2 * C] * c[:, 2 * C:]


def _hy_pre(p, w, b):
    B, L, W = p.shape
    tm = min(2048, L)
    C = D_HYENA
    return pl.pallas_call(
        _hy_pre_body,
        out_shape=(jax.ShapeDtypeStruct((B, L, C), F32),) * 2,
        grid=(B, L // tm),
        in_specs=[*_halo_specs(tm, W, L), _full((3, W)), _full((1, W))],
        out_specs=(pl.BlockSpec((1, tm, C), lambda b, i: (b, i, 0)),) * 2,
        compiler_params=_cparams("parallel", "parallel"),
        name="hy_pre",
    )(p, p, p, w, b.reshape(1, W))


FFT_TJ = 16


def _fft_a_body(u_ref, m_ref, twr_ref, twi_ref, o_ref, acc_ref, *, na, tj):
    reps = u_ref.shape[-1] // LANES
    for j in range(tj):
        a = _mm_dft(m_ref[...], u_ref[0, :, j, :])
        ar, ai = a[:na], a[na:]
        twr = jnp.concatenate([twr_ref[j]] * reps, axis=1)
        twi = jnp.concatenate([twi_ref[j]] * reps, axis=1)
        acc_ref[0, :, j, :] = ar * twr - ai * twi
        acc_ref[1, :, j, :] = ar * twi + ai * twr
    o_ref[0] = acc_ref[...].astype(BF16)


def _fft_a(u4, m, twr, twi):
    B, half, nb, C = u4.shape
    na = 2 * half
    tj = FFT_TJ
    tw = pl.BlockSpec((tj, na, LANES), lambda b, j: (j, 0, 0))
    return pl.pallas_call(
        functools.partial(_fft_a_body, na=na, tj=tj),
        out_shape=jax.ShapeDtypeStruct((B, 2, na, nb, C), BF16),
        grid=(B, nb // tj),
        in_specs=[pl.BlockSpec((1, half, tj, C), lambda b, j: (b, 0, j, 0)), _full((2 * na, half)), tw, tw],
        out_specs=pl.BlockSpec((1, 2, na, tj, C), lambda b, j: (b, 0, 0, j, 0)),
        scratch_shapes=[pltpu.VMEM((2, na, tj, C), F32)],
        compiler_params=_cparams("parallel", "parallel"),
        name="fft_a",
    )(u4, m, twr, twi)


def _fft_filt_body(bf_ref, bb_ref, m_ref, s_ref, o_ref, *, tk):
    nb = FFT_NB
    inv = 1.0 / s_ref[...]
    for k in range(tk):
        xf = _mm_dft(m_ref[...], jnp.concatenate([bf_ref[0, 0, k], bf_ref[0, 1, k]], axis=0))
        xb = _mm_dft(m_ref[...], jnp.concatenate([bb_ref[0, 0, k], bb_ref[0, 1, k]], axis=0))
        o_ref[0, k] = (xf[:nb] + xb[:nb]) * inv
        o_ref[1, k] = (xf[nb:] - xb[nb:]) * inv


def _fft_filt(bt, m, s):
    _, _, na, nb, C = bt.shape
    tk = min(4, na)
    return pl.pallas_call(
        functools.partial(_fft_filt_body, tk=tk),
        out_shape=jax.ShapeDtypeStruct((2, na, nb, C), F32),
        grid=(na // tk,),
        in_specs=[pl.BlockSpec((1, 2, tk, nb, C), lambda i: (0, 0, i, 0, 0)),
                  pl.BlockSpec((1, 2, tk, nb, C), lambda i: (1, 0, i, 0, 0)),
                  _full((2 * nb, 2 * nb)), _full((1, C))],
        out_specs=pl.BlockSpec((2, tk, nb, C), lambda i: (0, i, 0, 0)),
        compiler_params=_cparams("parallel"),
        name="fft_filt",
    )(bt, bt, m, s)


def _fft_c_body(b_ref, k_ref, mf_ref, mi_ref, twr_ref, twi_ref, o_ref, *, tk):
    nb = FFT_NB
    for k in range(tk):
        x = _mm_dft(mf_ref[...], jnp.concatenate([b_ref[0, 0, k], b_ref[0, 1, k]], axis=0))
        xr, xi = x[:nb], x[nb:]
        kr, ki = k_ref[0, k], k_ref[1, k]
        y = jnp.concatenate([xr * kr - xi * ki, xr * ki + xi * kr], axis=0)
        d = _mm_dft(mi_ref[...], y)
        dr, di = d[:nb], d[nb:]
        reps = dr.shape[1] // LANES
        twr = jnp.concatenate([twr_ref[k]] * reps, axis=1)
        twi = jnp.concatenate([twi_ref[k]] * reps, axis=1)
        o_ref[0, 0, k] = (dr * twr + di * twi).astype(BF16)
        o_ref[0, 1, k] = (di * twr - dr * twi).astype(BF16)


def _fft_c(bt, khat, mf, mi, twr, twi):
    B, _, na, nb, C = bt.shape
    tk = min(8, na)
    blk = pl.BlockSpec((1, 2, tk, nb, C), lambda i, b: (b, 0, i, 0, 0))
    m = _full((2 * nb, 2 * nb))
    tw = pl.BlockSpec((tk, nb, LANES), lambda i, b: (i, 0, 0))
    return pl.pallas_call(
        functools.partial(_fft_c_body, tk=tk),
        out_shape=jax.ShapeDtypeStruct(bt.shape, BF16),
        grid=(na // tk, B),
        in_specs=[blk, pl.BlockSpec((2, tk, nb, C), lambda i, b: (0, i, 0, 0)), m, m, tw, tw],
        out_specs=blk,
        compiler_params=_cparams("parallel", "parallel"),
        name="fft_c",
    )(bt, khat, mf, mi, twr, twi)


def _fft_out_body(e_ref, m_ref, u_ref, x0_ref, bias_ref, o_ref, ef_ref, acc_ref, *, tj):
    ef_ref[...] = e_ref[0].astype(F32)
    for j in range(tj):
        e = jnp.concatenate([ef_ref[0, :, j, :], ef_ref[1, :, j, :]], axis=0)
        conv = _mm_dft(m_ref[...], e)
        acc_ref[:, j, :] = x0_ref[0, :, j, :] * (conv + bias_ref[...] * u_ref[0, :, j, :])
    o_ref[0] = acc_ref[...].astype(BF16)


def _fft_out(e, m, u4, x04, bias):
    B, half, nb, C = u4.shape
    na = 2 * half
    tj = FFT_TJ
    blk = pl.BlockSpec((1, half, tj, C), lambda b, j: (b, 0, j, 0))
    return pl.pallas_call(
        functools.partial(_fft_out_body, tj=tj),
        out_shape=jax.ShapeDtypeStruct(u4.shape, BF16),
        grid=(B, nb // tj),
        in_specs=[pl.BlockSpec((1, 2, na, tj, C), lambda b, j: (b, 0, 0, j, 0)), _full((half, 2 * na)), blk, blk,
                  _full((1, C))],
        out_specs=blk,
        scratch_shapes=[pltpu.VMEM((2, na, tj, C), F32), pltpu.VMEM((half, tj, C), F32)],
        compiler_params=_cparams("parallel", "parallel"),
        name="fft_out",
    )(e, m, u4, x04, bias)


@functools.lru_cache(maxsize=None)
def _fft_consts_np(L):
    n = 2 * L
    nb = FFT_NB
    na = n // nb
    half = na // 2
    ka = np.arange(na)
    fa = np.exp(-2j * np.pi * np.outer(ka, ka) / na)
    kb = np.arange(nb)
    fb = np.exp(-2j * np.pi * np.outer(kb, kb) / nb)
    m_a = np.concatenate([fa.real[:, :half], fa.imag[:, :half]], axis=0)
    m_f = np.block([[fb.real, -fb.imag], [fb.imag, fb.real]])
    m_i = np.block([[fb.real, fb.imag], [-fb.imag, fb.real]])
    m_o = np.concatenate([fa.real[:half, :], fa.imag[:half, :]], axis=1) / n
    tw = np.exp(-2j * np.pi * (np.outer(ka, kb) % n) / n)
    f32 = lambda a: np.asarray(a, np.float32)
    return f32(m_a), f32(m_f), f32(m_i), f32(m_o), f32(tw.real), f32(tw.imag)


def _hyena_pos(L):
    t = jnp.linspace(0.0, 1.0, L, dtype=F32)[:, None]
    ang = 2.0 * math.pi * jnp.arange(L, dtype=F32)[:, None] / L
    bands = jnp.linspace(1e-4, HY_BANDS - 1, HY_BANDS, dtype=F32)[None, :]
    z = jnp.concatenate([t, jnp.cos(bands * ang), -jnp.sin(bands * ang)], axis=-1)
    return jnp.pad(z, ((0, 0), (0, LANES - HY_EMB))), t


def _hyena(p_hy, z, t, prm, l):
    B, L, _ = p_hy.shape
    C = D_HYENA
    nb = FFT_NB
    na = 2 * L // nb
    half = na // 2
    m_a, m_f, m_i, m_o, twr, twi = _fft_consts_np(L)
    ma, mf, mi, mo = (jnp.asarray(m) for m in (m_a, m_f, m_i, m_o))
    twr_l = jnp.broadcast_to(jnp.asarray(twr)[:, :, None], (na, nb, LANES))
    twi_l = jnp.broadcast_to(jnp.asarray(twi)[:, :, None], (na, nb, LANES))
    twr_t = jnp.broadcast_to(jnp.asarray(twr.T)[:, :, None], (nb, na, LANES))
    twi_t = jnp.broadcast_to(jnp.asarray(twi.T)[:, :, None], (nb, na, LANES))

    w1 = jnp.pad(prm["hy_w1"][l], ((0, LANES - HY_EMB), (0, 0)))
    hraw, hsum = _hy_filter(z, t, prm["hy_freq"][l], w1, prm["hy_b1"][l], prm["hy_w2"][l],
                            prm["hy_b2"][l], prm["hy_w3"][l], prm["hy_decay"][l])
    khat = _fft_filt(_fft_a(hraw.reshape(2, half, nb, C), ma, twr_t, twi_t), mf, hsum)

    u, x0 = _hy_pre(p_hy, prm["hy_conv_w"][l], prm["hy_conv_b"][l])
    u4 = u.reshape(B, half, nb, C)
    e = _fft_c(_fft_a(u4, ma, twr_t, twi_t), khat, mf, mi, twr_l, twi_l)
    y = _fft_out(e, mo, u4, x0.reshape(B, half, nb, C), prm["hy_bias"][l].reshape(1, C))
    return y.reshape(B * L, C)


def _head_block_mask():
    r = lax.broadcasted_iota(jnp.int32, (PAIR, PAIR), 0)
    c = lax.broadcasted_iota(jnp.int32, (PAIR, PAIR), 1)
    return ((r // HEAD_DIM) == (c // HEAD_DIM)).astype(F32)


def _bd(x, hm):
    return jnp.concatenate([x, x], axis=0) * hm


def _cat_masks(reverse):
    r = lax.broadcasted_iota(jnp.int32, (CHUNK, PAIR), 0)
    s = lax.broadcasted_iota(jnp.int32, (CHUNK, PAIR), 1) % CHUNK
    if reverse:
        return s > r, s >= r, s == r
    return s < r, s <= r, s == r


NEUMANN_PASSES = 1
GRAM_PASSES = 1


def _neumann_inverse(a, diag, hm):
    C = CHUNK
    t = jnp.where(diag, 1.0, 0.0) + a
    p = _mm(a, _bd(a, hm), passes=NEUMANN_PASSES)
    yield
    steps = int(math.log2(C)) - 1
    for _ in range(steps - 1):
        pt = _mm(jnp.concatenate([p, t], axis=0), _bd(p, hm), passes=NEUMANN_PASSES)
        yield
        p, t = pt[:C], t + pt[C:]
    return t + _mm(t, _bd(p, hm), passes=NEUMANN_PASSES)


def _interleave(gens):
    results = [None] * len(gens)
    alive = list(range(len(gens)))
    while alive:
        for i in list(alive):
            try:
                next(gens[i])
            except StopIteration as done:
                results[i] = done.value
                alive.remove(i)
    return results


def _rw_prep_body(p_ref, pp_ref, pn_ref, mup_ref, mun_ref, wl_ref, w0_ref, al_ref, a0_ref, gl_ref,
                  kk_ref, ka_ref, rk_ref, hs_ref,
                  v_o, gate_o, bonus_o, atf_o, rtf_o, khf_o, bhf_o, atb_o, rtb_o, khb_o, bhb_o, ptf_o, ptb_o):
    p = p_ref[0]
    tm = p.shape[0]
    prev_row, next_row = _halo_rows(pp_ref, pn_ref)
    prev, nxt = _shift_rows(p, prev_row, next_row)
    mup, mun = mup_ref[...], mun_ref[...]
    p = (1.0 - mup - mun) * p + mup * prev + mun * nxt
    D = D_RWKV
    r, k, v = p[:, 0:D], p[:, D:2 * D], p[:, 2 * D:3 * D]
    o = 3 * D
    lw = (p[:, o:o + RW_LORA_W], p[:, o + RW_LORA_W:o + 2 * RW_LORA_W])
    o += 2 * RW_LORA_W
    la = (p[:, o:o + RW_LORA_A], p[:, o + RW_LORA_A:o + 2 * RW_LORA_A])
    o += 2 * RW_LORA_A
    lg = p[:, o:o + RW_LORA_G]
    hs = hs_ref[...]
    gate_o[0] = _mm(_sigmoid(lg), gl_ref[...], passes=3)
    k2 = k * kk_ref[...]
    kkn = k2 * lax.rsqrt(_head_stat(k2 * k2, hs) + L2_EPS)
    v_o[0] = v.astype(BF16)
    bonus_o[0] = _head_stat(r * k * rk_ref[...], hs) * v
    outs = ((atf_o, rtf_o, khf_o, bhf_o, ptf_o), (atb_o, rtb_o, khb_o, bhb_o, ptb_o))
    cb = min(CUM_ROWS, tm)
    for d in range(2):
        at_o, rt_o, kh_o, bh_o, pt_o = outs[d]
        logw = -RW_DECAY_SCALE * _sigmoid(w0_ref[d:d + 1, :] + _mm(jnp.tanh(lw[d]), wl_ref[d], passes=3))
        a = _sigmoid(a0_ref[d:d + 1, :] + _mm(la[d], al_ref[d], passes=3))
        kd = k * (1.0 + (a - 1.0) * ka_ref[...])
        sel = jnp.concatenate([_chunk_cum_matrix(cb, bool(d)), _chunk_total_matrix(cb)], axis=0)
        for sb in range(tm // cb):
            rows = slice(sb * cb, (sb + 1) * cb)
            lw_s = logw[rows]
            ct = _mm_lconst(sel, lw_s)
            cum = ct[:cb]
            pinv = jnp.exp(-cum)
            at_o[0, rows, :] = (-kkn[rows] * jnp.exp(cum - lw_s)).astype(BF16)
            rt_o[0, rows, :] = (r[rows] * jnp.exp(cum)).astype(BF16)
            kh_o[0, rows, :] = (kd[rows] * pinv).astype(BF16)
            bh_o[0, rows, :] = (kkn[rows] * a[rows] * pinv).astype(BF16)
            pt_o[0, sb * cb // SUBLANES:(sb + 1) * cb // SUBLANES, :] = jnp.exp(ct[cb:])


CUM_ROWS = 256


def _chunk_cum_matrix(n, reverse):
    r = lax.broadcasted_iota(jnp.int32, (n, n), 0)
    c = lax.broadcasted_iota(jnp.int32, (n, n), 1)
    same = (r // CHUNK) == (c // CHUNK)
    return (same & ((c >= r) if reverse else (c <= r))).astype(F32)


def _chunk_total_matrix(n):
    r = lax.broadcasted_iota(jnp.int32, (n // SUBLANES, n), 0)
    c = lax.broadcasted_iota(jnp.int32, (n // SUBLANES, n), 1)
    return (r // (CHUNK // SUBLANES) == c // CHUNK).astype(F32)


def _rw_prep(p, prm, l):
    B, L, W = p.shape
    tm = min(512, L)
    D = D_RWKV
    row = lambda a: a.reshape(1, -1)
    out = pl.BlockSpec((1, tm, D), lambda b, i: (b, i, 0))
    tot = pl.BlockSpec((1, tm // SUBLANES, D), lambda b, i: (b, i, 0))
    full = jax.ShapeDtypeStruct((B, L, D), F32)
    half = jax.ShapeDtypeStruct((B, L, D), BF16)
    small = jax.ShapeDtypeStruct((B, L // SUBLANES, D), F32)
    return pl.pallas_call(
        _rw_prep_body,
        out_shape=(half, full, full) + (half,) * 8 + (small,) * 2,
        grid=(B, L // tm),
        in_specs=[*_halo_specs(tm, W, L), _full((1, W)), _full((1, W)),
                  _full((2, RW_LORA_W, D)), _full((2, D)), _full((2, RW_LORA_A, D)), _full((2, D)),
                  _full((RW_LORA_G, D)), _full((1, D)), _full((1, D)), _full((1, D)), _full((D, D))],
        out_specs=(out,) * 11 + (tot,) * 2,
        compiler_params=_cparams("parallel", "parallel"),
        name="rw_prep",
    )(p, p, p, row(prm["rw_mu_prev"][l]), row(prm["rw_mu_next"][l]), prm["rw_w_lora"][l], prm["rw_w0"][l],
      prm["rw_a_lora"][l], prm["rw_a0"][l], prm["rw_g_lora"][l], row(prm["rw_k_k"][l]),
      row(prm["rw_k_a"][l]), row(prm["rw_r_k"][l]), _head_sum_matrix(D, RW_HEAD_DIM))


def _rw_chunk(v, at, rt, kh, bh, ptot, s_in, s_out, reverse, hm):
    C = CHUNK
    strict, incl, diag = _cat_masks(reverse)
    kb_bd = jnp.concatenate([_bd(kh, hm), _bd(bh, hm)], axis=0)
    g = _mm(jnp.concatenate([at, rt], axis=0), kb_bd, NT, passes=GRAM_PASSES)
    yield
    n = 2 * C
    a_ak = jnp.where(strict, g[:C, :n], 0.0)
    a_ab = jnp.where(strict, g[:C, n:], 0.0)
    a_rk = jnp.where(incl, g[C:, :n], 0.0)
    a_rb = jnp.where(incl, g[C:, n:], 0.0)
    av = _mm(jnp.concatenate([a_ak, a_rk], axis=0), _bd(v, hm))
    t = yield from _neumann_inverse(a_ab, diag, hm)
    yield
    uw = _mm(t, jnp.concatenate([_bd(av[:C], hm), _bd(at, hm)], axis=1))
    yield
    while s_in[0] is None:
        yield
    s0 = s_in[0]
    ws = _mm(jnp.concatenate([uw[:, PAIR:], rt], axis=0), s0, NT)
    yield
    u = uw[:, :PAIR] + ws[:C]
    upd = _mm(jnp.concatenate([v, u], axis=0), jnp.concatenate([kh, bh], axis=0), TN) * hm
    s_out[0] = (s0 + upd) * ptot
    return ws[C:] + av[C:] + _mm(a_rb, _bd(u, hm))


def _scan_schedule(n_pairs, make_chain, s_ref):
    gens, outs, finals = [], [], []
    for c in range(CHUNKS_PER_STEP):
        for d in range(2):
            ci = CHUNKS_PER_STEP - 1 - c if d else c
            rows = slice(ci * CHUNK, (ci + 1) * CHUNK)
            for j in range(n_pairs):
                if c == 0:
                    finals.append([s_ref[d, j]])
                k = d * n_pairs + j
                s_in = finals[k]
                s_out = [None]
                finals[k] = s_out
                gens.append(make_chain(d, j, rows, s_in, s_out))
                outs.append((d, j, rows))
    return gens, outs, finals


def _rw_scan_body(vf, atf, rtf, khf, bhf, ptf, vb, atb, rtb, khb, bhb, ptb, yf_o, yb_o, s_ref):
    @pl.when(pl.program_id(1) == 0)
    def _():
        s_ref[...] = jnp.zeros(s_ref.shape, F32)

    hm = _head_block_mask()
    dirs = ((vf, atf, rtf, khf, bhf, ptf, yf_o), (vb, atb, rtb, khb, bhb, ptb, yb_o))
    n_pairs = D_RWKV // PAIR

    def make_chain(d, j, rows, s_in, s_out):
        sl = slice(j * PAIR, (j + 1) * PAIR)
        v, at, rt, kh, bh, pt, _ = dirs[d]
        t0 = rows.start // SUBLANES
        return _rw_chunk(v[0, rows, sl], at[0, rows, sl], rt[0, rows, sl], kh[0, rows, sl], bh[0, rows, sl],
                         pt[0, t0:t0 + 1, sl], s_in, s_out, bool(d), hm)

    gens, outs, finals = _scan_schedule(n_pairs, make_chain, s_ref)
    for y, (d, j, rows) in zip(_interleave(gens), outs):
        dirs[d][-1][0, rows, j * PAIR:(j + 1) * PAIR] = y.astype(BF16)
    for k, cell in enumerate(finals):
        s_ref[k // n_pairs, k % n_pairs] = cell[0]


def _rw_scan(v, atf, rtf, khf, bhf, ptf, atb, rtb, khb, bhb, ptb):
    B, L, D = v.shape
    blk = CHUNKS_PER_STEP * CHUNK
    n = L // blk
    fw = pl.BlockSpec((1, blk, D), lambda b, i: (b, i, 0))
    bw = pl.BlockSpec((1, blk, D), lambda b, i: (b, n - 1 - i, 0))
    fwt = pl.BlockSpec((1, blk // SUBLANES, D), lambda b, i: (b, i, 0))
    bwt = pl.BlockSpec((1, blk // SUBLANES, D), lambda b, i: (b, n - 1 - i, 0))
    return pl.pallas_call(
        _rw_scan_body,
        out_shape=(jax.ShapeDtypeStruct((B, L, D), BF16),) * 2,
        grid=(B, n),
        in_specs=[fw] * 5 + [fwt] + [bw] * 5 + [bwt],
        out_specs=(fw, bw),
        scratch_shapes=[pltpu.VMEM((2, D // PAIR, PAIR, PAIR), F32)],
        compiler_params=_cparams("parallel", "arbitrary"),
        name="rw_scan",
    )(v, atf, rtf, khf, bhf, ptf, v, atb, rtb, khb, bhb, ptb)


GDN_G_LANE = 0
GDN_BETA_LANE = 2 * GDN_HEADS


def _gdn_prep_body(p_ref, pp_ref, pn_ref, s_ref, w_ref, alog_ref, dt_ref, hs_ref,
                   q_o, k_o, v_o, gcf_o, gcb_o, bf_o, bb_o):
    p = p_ref[0]
    tm = p.shape[0]
    prev_row, next_row = _halo_rows(pp_ref, pn_ref)
    prev, nxt = _shift_rows(p, prev_row, next_row)
    c = _silu(w_ref[0:1, :] * prev + w_ref[1:2, :] * p + w_ref[2:3, :] * nxt)
    D = D_GDN
    q, k, v = c[:, 0:D], c[:, D:2 * D], c[:, 2 * D:3 * D]
    hs = hs_ref[...]
    q_o[0] = q * lax.rsqrt(_head_stat(q * q, hs) + L2_EPS) * (GDN_HEAD_DIM ** -0.5)
    k_o[0] = k * lax.rsqrt(_head_stat(k * k, hs) + L2_EPS)
    v_o[0] = v
    s = s_ref[0]
    lane = lax.broadcasted_iota(jnp.int32, s.shape, 1)
    g = -jnp.exp(alog_ref[...]) * _softplus(s + dt_ref[...])
    gs = jnp.where(lane < GDN_BETA_LANE, g, jnp.where(lane < 4 * GDN_HEADS, _sigmoid(s), 0.0))
    er = lax.broadcasted_iota(jnp.int32, (LANES, 4 * D), 0)
    ec = lax.broadcasted_iota(jnp.int32, (LANES, 4 * D), 1)
    expand = (er == (ec // D) * GDN_HEADS + (ec % D) // GDN_HEAD_DIM).astype(F32)
    cb = min(CUM_ROWS, tm)
    lane_s = lax.broadcasted_iota(jnp.int32, (cb, LANES), 1)
    for sb in range(tm // cb):
        rows = slice(sb * cb, (sb + 1) * cb)
        g_s = gs[rows]
        gc = jnp.where(lane_s < GDN_HEADS, _mm_lconst(_chunk_cum_matrix(cb, False), g_s),
                       jnp.where(lane_s < GDN_BETA_LANE, _mm_lconst(_chunk_cum_matrix(cb, True), g_s), g_s))
        gx = _mm_rconst(gc, expand)
        gcf_o[0, rows, :] = gx[:, 0:D]
        gcb_o[0, rows, :] = gx[:, D:2 * D]
        bf_o[0, rows, :] = gx[:, 2 * D:3 * D]
        bb_o[0, rows, :] = gx[:, 3 * D:]


def _gdn_prep(p, prm, l):
    B, L, W = p.shape
    tm = min(512, L)
    D = D_GDN
    Wq = 3 * D
    pad = LANES - 2 * GDN_HEADS
    alog = jnp.pad(prm["gdn_a_log"][l].reshape(-1), (0, pad)).reshape(1, LANES)
    dt = jnp.pad(prm["gdn_dt_bias"][l].reshape(-1), (0, pad)).reshape(1, LANES)
    out = pl.BlockSpec((1, tm, D), lambda b, i: (b, i, 0))
    side = pl.BlockSpec((1, tm, LANES), lambda b, i: (b, i, 4 * D // LANES))
    return pl.pallas_call(
        _gdn_prep_body,
        out_shape=(jax.ShapeDtypeStruct((B, L, D), F32),) * 7,
        grid=(B, L // tm),
        in_specs=[*_halo_specs(tm, Wq, L), side, _full((3, Wq)), _full((1, LANES)), _full((1, LANES)),
                  _full((D, D))],
        out_specs=(out,) * 7,
        compiler_params=_cparams("parallel", "parallel"),
        name="gdn_prep",
    )(p, p, p, p, prm["gdn_conv_w"][l], alog, dt, _head_sum_matrix(D, GDN_HEAD_DIM))


def _gdn_chunk(q, k, v, gcx, beta, s_in, s_out, reverse, hm):
    C = CHUNK
    strict, incl, diag = _cat_masks(reverse)
    grow = _mm_lconst(jnp.ones((C, C), F32), jnp.where(diag, gcx, 0.0))
    kb = k * beta
    kq = _mm(jnp.concatenate([kb, q], axis=0), _bd(k, hm), NT, passes=GRAM_PASSES)
    yield
    dec = jnp.exp(jnp.where(incl, gcx - grow, -1e30))
    a = jnp.where(strict, kq[:C] * dec, 0.0)
    attn = kq[C:] * dec
    t = yield from _neumann_inverse(-a, diag, hm)
    yield
    egc = jnp.exp(gcx)
    sol = _mm(t, jnp.concatenate([_bd(v * beta, hm), _bd(kb * egc, hm)], axis=1))
    yield
    while s_in[0] is None:
        yield
    s0 = s_in[0]
    ws = _mm(jnp.concatenate([sol[:, PAIR:], q * egc], axis=0), s0)
    yield
    v_new = sol[:, :PAIR] - ws[:C]
    glast = gcx[0:1] if reverse else gcx[C - 1:C]
    upd = _mm(k * jnp.exp(glast - gcx), v_new, TN) * hm
    s_out[0] = s0 * jnp.exp(glast) + upd
    return ws[C:] + _mm(attn, _bd(v_new, hm))


def _gdn_scan_body(qf, kf, vf, gf, btf, qb, kb, vb, gb, btb, of_o, ob_o, s_ref):
    @pl.when(pl.program_id(1) == 0)
    def _():
        s_ref[...] = jnp.zeros(s_ref.shape, F32)

    hm = _head_block_mask()
    dirs = ((qf, kf, vf, gf, btf, of_o), (qb, kb, vb, gb, btb, ob_o))
    n_pairs = D_GDN // PAIR

    def make_chain(d, j, rows, s_in, s_out):
        sl = slice(j * PAIR, (j + 1) * PAIR)
        q, k, v, g, bt, _ = dirs[d]
        return _gdn_chunk(q[0, rows, sl], k[0, rows, sl], v[0, rows, sl], g[0, rows, sl], bt[0, rows, sl],
                          s_in, s_out, bool(d), hm)

    gens, outs, finals = _scan_schedule(n_pairs, make_chain, s_ref)
    for o, (d, j, rows) in zip(_interleave(gens), outs):
        dirs[d][-1][0, rows, j * PAIR:(j + 1) * PAIR] = o.astype(BF16)
    for k, cell in enumerate(finals):
        s_ref[k // n_pairs, k % n_pairs] = cell[0]


def _gdn_scan(q, k, v, gcf, gcb, btf, btb):
    B, L, D = q.shape
    blk = CHUNKS_PER_STEP * CHUNK
    n = L // blk
    fw = pl.BlockSpec((1, blk, D), lambda b, i: (b, i, 0))
    bw = pl.BlockSpec((1, blk, D), lambda b, i: (b, n - 1 - i, 0))
    return pl.pallas_call(
        _gdn_scan_body,
        out_shape=(jax.ShapeDtypeStruct((B, L, D), BF16),) * 2,
        grid=(B, n),
        in_specs=[fw] * 5 + [bw] * 5,
        out_specs=(fw, bw),
        scratch_shapes=[pltpu.VMEM((2, D // PAIR, PAIR, PAIR), F32)],
        compiler_params=_cparams("parallel", "arbitrary"),
        name="gdn_scan",
    )(q, k, v, gcf, btf, q, k, v, gcb, btb)


def _mix_out_body(x_ref, yhy_ref, yf_ref, yb_ref, gate_ref, bonus_ref, of_ref, ob_ref, zg_ref,
                  gnw_ref, gnb_ref, nw_ref, avg_ref, w_ref, o_ref):
    avg = avg_ref[...]
    y = yf_ref[...].astype(F32) + yb_ref[...].astype(F32)
    mu = _head_stat(y, avg)
    dlt = y - mu
    var = _head_stat(dlt * dlt, avg)
    y_rw = (dlt * lax.rsqrt(var + RW_GN_EPS) * gnw_ref[...] + gnb_ref[...] + bonus_ref[...]) * gate_ref[...]
    o = of_ref[...].astype(F32) + ob_ref[...].astype(F32)
    ms = _head_stat(o * o, avg)
    y_gdn = o * lax.rsqrt(ms + NORM_EPS) * nw_ref[...] * _silu(zg_ref[...])
    c0, c1 = D_HYENA, D_HYENA + D_RWKV
    acc = _dg(yhy_ref[...].astype(BF16), w_ref[0:c0, :], NN)
    acc += _dg(y_rw.astype(BF16), w_ref[c0:c1, :], NN)
    acc += _dg(y_gdn.astype(BF16), w_ref[c1:, :], NN)
    o_ref[...] = x_ref[...] + acc


def _mix_out(x, y_hy, yf, yb, gate, bonus, of, ob, p_gdn, prm, l, w_out):
    T, D = x.shape
    tm = min(512, T)
    Dh = D_RWKV
    row = lambda a: a.reshape(1, -1)
    tile = lambda n: pl.BlockSpec((tm, n), lambda i: (i, 0))
    zg = pl.BlockSpec((tm, D_GDN), lambda i: (i, 3))
    avg = _head_sum_matrix(Dh, HEAD_DIM, 1.0 / HEAD_DIM)
    return pl.pallas_call(
        _mix_out_body,
        out_shape=jax.ShapeDtypeStruct((T, D), F32),
        grid=(T // tm,),
        in_specs=[tile(D), tile(D_HYENA)] + [tile(Dh)] * 6 + [zg, _full((1, Dh)), _full((1, Dh)),
                  _full((1, Dh)), _full((Dh, Dh)), _full((D, D))],
        out_specs=tile(D),
        compiler_params=_cparams("parallel"),
        name="mix_out",
    )(x, y_hy, yf, yb, gate, bonus, of, ob, p_gdn, row(prm["rw_gn_w"][l]), row(prm["rw_gn_b"][l]),
      row(jnp.tile(prm["gdn_norm_w"][l], GDN_HEADS)), avg, w_out)


def _mem_kv_body(m_ref, g_ref, wk_ref, wv_ref, k_o, v_o):
    h = _rms(m_ref[0], g_ref[...]).astype(BF16)
    k_o[0] = _dg(h, wk_ref[...], NN).astype(BF16)
    v_o[0] = _dg(h, wv_ref[...], NN).astype(BF16)


def _mem_kv(mem, g, wk, wv):
    B, M, D = mem.shape
    blk = pl.BlockSpec((1, M, D), lambda b: (b, 0, 0))
    return pl.pallas_call(
        _mem_kv_body,
        out_shape=(jax.ShapeDtypeStruct((B, M, D), BF16),) * 2,
        grid=(B,),
        in_specs=[blk, _full((1, D)), _full((D, D)), _full((D, D))],
        out_specs=(blk, blk),
        compiler_params=_cparams("parallel"),
        name="mem_kv",
    )(mem, g.reshape(1, D), wk, wv)


def _xattn_body(x_ref, g_ref, wq_ref, k_ref, v_ref, wo_ref, o_ref):
    x = x_ref[0]
    h = _rms(x, g_ref[...]).astype(BF16)
    q = (_dg(h, wq_ref[...], NN) * (XA_HEAD_DIM ** -0.5)).astype(BF16)
    outs = []
    for hd in range(XA_HEADS):
        sl = slice(hd * XA_HEAD_DIM, (hd + 1) * XA_HEAD_DIM)
        s = _dg(q[:, sl], k_ref[0, :, sl], NT)
        s = s - jnp.max(s, axis=-1, keepdims=True)
        e = jnp.exp(s)
        inv = 1.0 / jnp.sum(e, axis=-1, keepdims=True)
        outs.append(_dg(e.astype(BF16), v_ref[0, :, sl], NN) * inv)
    o = jnp.concatenate(outs, axis=1).astype(BF16)
    o_ref[0] = x + _dg(o, wo_ref[...], NN)


def _xattn(x, g, wq, k, v, wo):
    B, L, D = x.shape
    M = k.shape[1]
    tm = min(512, L)
    tile = pl.BlockSpec((1, tm, D), lambda b, i: (b, i, 0))
    kv = pl.BlockSpec((1, M, D), lambda b, i: (b, 0, 0))
    return pl.pallas_call(
        _xattn_body,
        out_shape=jax.ShapeDtypeStruct((B, L, D), F32),
        grid=(B, L // tm),
        in_specs=[tile, _full((1, D)), _full((D, D)), kv, kv, _full((D, D))],
        out_specs=tile,
        compiler_params=_cparams("parallel", "parallel"),
        name="xattn",
    )(x, g.reshape(1, D), wq, k, v, wo)


def kernel(x, mem, norm_ffn1, ffn1_w1, ffn1_w3, ffn1_w2, norm_mix, w_in, w_out, hy_conv_w, hy_conv_b, hy_freq, hy_w1, hy_b1, hy_w2, hy_b2, hy_w3, hy_decay, hy_bias, rw_mu_prev, rw_mu_next, rw_w_lora, rw_w0, rw_a_lora, rw_a0, rw_g_lora, rw_k_k, rw_k_a, rw_r_k, rw_gn_w, rw_gn_b, gdn_conv_w, gdn_a_log, gdn_dt_bias, gdn_norm_w, norm_xattn, xa_wq, xa_wk, xa_wv, xa_wo, mem_norm, norm_ffn2, ffn2_w1, ffn2_w3, ffn2_w2, norm_final):
    prm = dict(hy_conv_w=hy_conv_w, hy_conv_b=hy_conv_b, hy_freq=hy_freq, hy_w1=hy_w1, hy_b1=hy_b1,
               hy_w2=hy_w2, hy_b2=hy_b2, hy_w3=hy_w3, hy_decay=hy_decay, hy_bias=hy_bias,
               rw_mu_prev=rw_mu_prev, rw_mu_next=rw_mu_next, rw_w_lora=rw_w_lora, rw_w0=rw_w0,
               rw_a_lora=rw_a_lora, rw_a0=rw_a0, rw_g_lora=rw_g_lora, rw_k_k=rw_k_k, rw_k_a=rw_k_a,
               rw_r_k=rw_r_k, rw_gn_w=rw_gn_w, rw_gn_b=rw_gn_b, gdn_conv_w=gdn_conv_w,
               gdn_a_log=gdn_a_log, gdn_dt_bias=gdn_dt_bias, gdn_norm_w=gdn_norm_w)
    B, L, D = x.shape
    depth = norm_ffn1.shape[0]
    T = B * L
    z_pos, t_pos = _hyena_pos(L)
    bf = lambda w: w.astype(BF16)
    w_in_p = jnp.pad(w_in, ((0, 0), (0, 0), (0, GDN_COLS_PAD - GDN_COLS)))
    xt = x.reshape(T, D)
    for l in range(depth):
        xt = _ffn(xt, norm_ffn1[l], bf(ffn1_w1[l]), bf(ffn1_w3[l]), bf(ffn1_w2[l]), norm_final, False)
        p_hy, p_rw, p_gdn = _inproj(xt, norm_mix[l], bf(w_in_p[l]))
        y_hy = _hyena(p_hy.reshape(B, L, -1), z_pos, t_pos, prm, l)
        v, gate, bonus, *rw_ops = _rw_prep(p_rw.reshape(B, L, -1), prm, l)
        yf, yb = _rw_scan(v, *rw_ops[0:4], rw_ops[8], *rw_ops[4:8], rw_ops[9])
        of, ob = _gdn_scan(*_gdn_prep(p_gdn.reshape(B, L, -1), prm, l))
        flat = lambda a: a.reshape(T, -1)
        xt = _mix_out(xt, y_hy, flat(yf), flat(yb), flat(gate), flat(bonus), flat(of), flat(ob), p_gdn,
                      prm, l, bf(w_out[l]))
        km, vm = _mem_kv(mem, mem_norm, bf(xa_wk[l]), bf(xa_wv[l]))
        xt = _xattn(xt.reshape(B, L, D), norm_xattn[l], bf(xa_wq[l]), km, vm, bf(xa_wo[l])).reshape(T, D)
        xt = _ffn(xt, norm_ffn2[l], bf(ffn2_w1[l]), bf(ffn2_w3[l]), bf(ffn2_w2[l]), norm_final,
                  l == depth - 1)
    return xt.reshape(B, L, D)
```
